```python
import jax, jax.numpy as jnp
from jax import lax
import numpy as np

D_MODEL = 2048
BATCH = 1
SEQ = 8192
DEPTH = 4

GRID_W = 64
CTX_LEN = 256
EPS = 1e-6
N_MOD = 6
N_BRANCH = 4
BRANCH_W = D_MODEL // 4
A_HEADS = 8
A_KV_HEADS = 2
A_HEAD_DIM = BRANCH_W // A_HEADS
A_WINDOW = 128
A_BLOCK = 128
ROPE_THETA = 10000.0
B_HEADS = 8
B_HEAD_DIM = BRANCH_W // B_HEADS
NB_ROWS = 8
NB_COLS = 16
C_WIDTH = BRANCH_W
C_BLOCKS = 4
C_BLOCK_W = C_WIDTH // C_BLOCKS
C_CONV = 4
C_POW = 8.0
D_WIDTH = BRANCH_W
D_CONV = 3
FFN_HIDDEN = -(-(8 * D_MODEL) // (3 * 256)) * 256
COL_SIZES = (A_HEADS * A_HEAD_DIM, A_KV_HEADS * A_HEAD_DIM, A_KV_HEADS * A_HEAD_DIM,
             BRANCH_W, BRANCH_W, BRANCH_W,
             C_WIDTH, C_WIDTH,
             D_WIDTH, D_WIDTH, D_WIDTH,
             N_BRANCH * D_MODEL)
IN_COLS = sum(COL_SIZES)

kernel_name = "hybrid_parallel_dit_block"

F32 = jnp.float32


def _rmsnorm(x, g):
    x32 = x.astype(F32)
    y = x32 * lax.rsqrt(jnp.mean(x32 * x32, axis=-1, keepdims=True) + EPS)
    return y.astype(x.dtype) * g


def _modulate(xn, shift, scale):
    return xn * (1 + scale) + shift


def _split_cols(z):
    return jnp.split(z, np.cumsum(COL_SIZES)[:-1].tolist(), axis=-1)


def _heads(z, n, d):
    return z.reshape(z.shape[0], z.shape[1], n, d)


def _rope_1d(x, pos):
    half = x.shape[-1] // 2
    freqs = ROPE_THETA ** (-jnp.arange(half, dtype=F32) / half)
    ang = pos.astype(F32)[:, None] * freqs[None, :]
    cos = jnp.cos(ang)[None, :, None, :]
    sin = jnp.sin(ang)[None, :, None, :]
    x32 = x.astype(F32)
    x1, x2 = x32[..., :half], x32[..., half:]
    return jnp.concatenate([x1 * cos - x2 * sin, x1 * sin + x2 * cos], axis=-1).astype(x.dtype)


def _rope_2d(x, row, col):
    d = x.shape[-1] // 2
    return jnp.concatenate([_rope_1d(x[..., :d], row), _rope_1d(x[..., d:], col)], axis=-1)


def _dwconv(x, w, left):
    K, T = w.shape[0], x.shape[1]
    xp = jnp.pad(x, ((0, 0), (left, K - 1 - left), (0, 0)))
    out = xp[:, 0:T] * w[0]
    for j in range(1, K):
        out = out + xp[:, j:j + T] * w[j]
    return out


def _window_gqa(q, k, v, qc, kc, vc, sink, with_ctx):
    bsz, seq = q.shape[0], q.shape[1]
    nb = seq // A_BLOCK
    grp = A_HEADS // A_KV_HEADS
    scale = A_HEAD_DIM ** -0.5
    qb = q.reshape(bsz, nb, A_BLOCK, A_KV_HEADS, grp, A_HEAD_DIM)

    def band(t):
        tp = jnp.pad(t, ((0, 0), (A_BLOCK, A_BLOCK), (0, 0), (0, 0)))
        tp = tp.reshape(bsz, nb + 2, A_BLOCK, A_KV_HEADS, A_HEAD_DIM)
        return jnp.concatenate([tp[:, :-2], tp[:, 1:-1], tp[:, 2:]], axis=2)

    kb, vb = band(k), band(v)
    qpos = jnp.arange(seq).reshape(nb, A_BLOCK)
    kpos = (jnp.arange(nb) * A_BLOCK - A_BLOCK)[:, None] + jnp.arange(3 * A_BLOCK)[None, :]
    mask = ((jnp.abs(qpos[:, :, None] - kpos[:, None, :]) <= A_WINDOW)
            & (kpos >= 0)[:, None, :] & (kpos < seq)[:, None, :])
    s_loc = jnp.einsum('bnqhgd,bnkhd->bnhgqk', qb, kb).astype(F32) * scale
    s_loc = jnp.where(mask[None, :, None, None], s_loc, -jnp.inf)
    s_ctx = jnp.einsum('bnqhgd,blhd->bnhgql', qb, kc).astype(F32) * scale
    sink_g = sink.astype(F32).reshape(A_KV_HEADS, grp)
    s_sink = jnp.broadcast_to(sink_g[None, None, :, :, None, None], s_loc.shape[:-1] + (1,))
    p = jax.nn.softmax(jnp.concatenate([s_loc, s_ctx, s_sink], axis=-1), axis=-1).astype(v.dtype)
    nk, nc = 3 * A_BLOCK, kc.shape[1]
    o = (jnp.einsum('bnhgqk,bnkhd->bnqhgd', p[..., :nk], vb)
         + jnp.einsum('bnhgql,blhd->bnqhgd', p[..., nk:nk + nc], vc))
    o = o.reshape(bsz, seq, A_HEADS * A_HEAD_DIM)
    if not with_ctx:
        return o, None
    L = qc.shape[1]
    qcg = qc.reshape(bsz, L, A_KV_HEADS, grp, A_HEAD_DIM)
    sc = jnp.einsum('blhgd,bmhd->bhglm', qcg, kc).astype(F32) * scale
    sc_sink = jnp.broadcast_to(sink_g[None, :, :, None, None], sc.shape[:-1] + (1,))
    pc = jax.nn.softmax(jnp.concatenate([sc, sc_sink], axis=-1), axis=-1).astype(vc.dtype)
    oc = jnp.einsum('bhglm,bmhd->blhgd', pc[..., :L], vc).reshape(bsz, L, A_HEADS * A_HEAD_DIM)
    return o, oc


def _neighbourhood_attn(q, k, v, qc, kc, vc, rel_bias, with_ctx):
    bsz, seq = q.shape[0], q.shape[1]
    rows = seq // GRID_W
    wr = min(NB_ROWS, rows)
    scale = B_HEAD_DIM ** -0.5
    r = jnp.arange(rows)
    row_idx = jnp.clip(r - wr // 2, 0, rows - wr)[:, None] + jnp.arange(wr)[None, :]
    cq = jnp.arange(GRID_W)
    cstart = jnp.clip(cq - NB_COLS // 2, 0, GRID_W - NB_COLS)
    col_ok = (cq[None, :] >= cstart[:, None]) & (cq[None, :] < cstart[:, None] + NB_COLS)
    dr = row_idx - r[:, None] + NB_ROWS - 1
    dc = jnp.clip(cq[None, :] - cq[:, None] + NB_COLS - 1, 0, 2 * NB_COLS - 2)
    bias = rel_bias.astype(F32)[:, dr[:, None, :, None], dc[None, :, None, :]]
    q5 = q.reshape(bsz, rows, GRID_W, B_HEADS, B_HEAD_DIM)

    def gather_rows(t):
        return t.reshape(bsz, rows, GRID_W, B_HEADS, B_HEAD_DIM)[:, row_idx]

    k6, v6 = gather_rows(k), gather_rows(v)
    s_loc = jnp.einsum('brqhd,brikhd->bhrqik', q5, k6).astype(F32) * scale + bias[None]
    s_loc = jnp.where(col_ok[:, None, :], s_loc, -jnp.inf).reshape(bsz, B_HEADS, rows, GRID_W, wr * GRID_W)
    s_ctx = jnp.einsum('brqhd,blhd->bhrql', q5, kc).astype(F32) * scale
    p = jax.nn.softmax(jnp.concatenate([s_loc, s_ctx], axis=-1), axis=-1).astype(v.dtype)
    nk = wr * GRID_W
    p_loc = p[..., :nk].reshape(bsz, B_HEADS, rows, GRID_W, wr, GRID_W)
    o = (jnp.einsum('bhrqik,brikhd->brqhd', p_loc, v6)
         + jnp.einsum('bhrql,blhd->brqhd', p[..., nk:], vc))
    o = o.reshape(bsz, seq, B_HEADS * B_HEAD_DIM)
    if not with_ctx:
        return o, None
    L = qc.shape[1]
    sc = jnp.einsum('blhd,bmhd->bhlm', qc, kc).astype(F32) * scale
    pc = jax.nn.softmax(sc, axis=-1).astype(vc.dtype)
    oc = jnp.einsum('bhlm,bmhd->blhd', pc, vc).reshape(bsz, L, B_HEADS * B_HEAD_DIM)
    return o, oc


def _rglru_gates(u, w_a, b_a, w_x, b_x, lam):
    bsz, T, _ = u.shape
    u32 = u.astype(F32)
    ub = u32.reshape(bsz, T, C_BLOCKS, C_BLOCK_W)
    r = jax.nn.sigmoid(jnp.einsum('btnc,ncd->btnd', ub, w_a.astype(F32)).reshape(bsz, T, C_WIDTH) + b_a.astype(F32))
    i = jax.nn.sigmoid(jnp.einsum('btnc,ncd->btnd', ub, w_x.astype(F32)).reshape(bsz, T, C_WIDTH) + b_x.astype(F32))
    log_a = -C_POW * r * jax.nn.softplus(-lam.astype(F32))
    return jnp.exp(log_a), jnp.sqrt(-jnp.expm1(2.0 * log_a)) * (i * u32)


def _linear_scan(a, b, h0, reverse):
    def combine(e1, e2):
        return e1[0] * e2[0], e2[0] * e1[1] + e2[1]
    acum, bcum = lax.associative_scan(combine, (a, b), axis=1, reverse=reverse)
    return acum * h0[:, None, :] + bcum


def _rglru_bidir(xr, xrc, conv_w, conv_b, w_a, b_a, w_x, b_x, lam, with_ctx):
    u = _dwconv(xr, conv_w, C_CONV // 2) + conv_b
    uc = _dwconv(xrc, conv_w, C_CONV // 2) + conv_b
    lat, ctxs = [], []
    for d in range(2):
        rev = d == 1
        ac, bc = _rglru_gates(uc, w_a[d], b_a[d], w_x[d], b_x[d], lam[d])
        sc = _linear_scan(ac, bc, jnp.zeros_like(bc[:, 0]), rev)
        h0 = sc[:, 0] if rev else sc[:, -1]
        a, b = _rglru_gates(u, w_a[d], b_a[d], w_x[d], b_x[d], lam[d])
        lat.append(_linear_scan(a, b, h0, rev))
        ctxs.append(sc)
    h_lat = (lat[0] + lat[1]).astype(xr.dtype)
    h_ctx = (ctxs[0] + ctxs[1]).astype(xrc.dtype) if with_ctx else None
    return h_lat, h_ctx


def _merge(branches, gate_logits, b_gate, w_branch, w_out):
    g = jax.nn.sigmoid(gate_logits.reshape(gate_logits.shape[:-1] + (N_BRANCH, D_MODEL)) + b_gate)
    merged = g[..., 0, :] * (branches[0] @ w_branch[0])
    for k in range(1, N_BRANCH):
        merged = merged + g[..., k, :] * (branches[k] @ w_branch[k])
    return merged @ w_out


def _token_mixers(h, hc, row, col, w_in, b_gate, a_sink, nb_bias, c_conv_w, c_conv_b,
                  c_w_a, c_b_a, c_w_x, c_b_x, c_lam, d_conv_w, w_branch, w_out, with_ctx):
    qa, ka, va, qb, kb, vb, xr, gr, xd, bd, cd, gt = _split_cols(h @ w_in)
    qac, kac, vac, qbc, kbc, vbc, xrc, grc, xdc, bdc, cdc, gtc = _split_cols(hc @ w_in)
    ya, yac = _window_gqa(_rope_2d(_heads(qa, A_HEADS, A_HEAD_DIM), row, col),
                          _rope_2d(_heads(ka, A_KV_HEADS, A_HEAD_DIM), row, col),
                          _heads(va, A_KV_HEADS, A_HEAD_DIM),
                          _heads(qac, A_HEADS, A_HEAD_DIM), _heads(kac, A_KV_HEADS, A_HEAD_DIM),
                          _heads(vac, A_KV_HEADS, A_HEAD_DIM), a_sink, with_ctx)
    yb, ybc = _neighbourhood_attn(_heads(qb, B_HEADS, B_HEAD_DIM), _heads(kb, B_HEADS, B_HEAD_DIM),
                                  _heads(vb, B_HEADS, B_HEAD_DIM), _heads(qbc, B_HEADS, B_HEAD_DIM),
                                  _heads(kbc, B_HEADS, B_HEAD_DIM), _heads(vbc, B_HEADS, B_HEAD_DIM),
                                  nb_bias, with_ctx)
    hr, hrc = _rglru_bidir(xr, xrc, c_conv_w, c_conv_b, c_w_a, c_b_a, c_w_x, c_b_x, c_lam, with_ctx)
    yr = hr * jax.nn.gelu(gr)
    yd = bd * _dwconv(cd * xd, d_conv_w, D_CONV // 2)
    y = _merge([ya, yb, yr, yd], gt, b_gate, w_branch, w_out)
    if not with_ctx:
        return y, None
    yrc = hrc * jax.nn.gelu(grc)
    ydc = bdc * _dwconv(cdc * xdc, d_conv_w, D_CONV // 2)
    yc = _merge([yac, ybc, yrc, ydc], gtc, b_gate, w_branch, w_out)
    return y, yc


def _swiglu(h, w_ffn_in, w_ffn_out):
    g, u = jnp.split(h @ w_ffn_in, 2, axis=-1)
    return (jax.nn.silu(g) * u) @ w_ffn_out


def setup_inputs(seed: int = 0) -> dict:
    key = jax.random.key(seed)
    ks = jax.random.split(key, 26)
    n = jax.random.normal
    D = D_MODEL
    lam_u = jax.random.uniform(ks[16], (DEPTH, 2, C_WIDTH), F32, 0.9, 0.999)
    lam_s = lam_u ** (1.0 / C_POW)
    return {
        "x": n(ks[0], (BATCH, SEQ, D), F32),
        "c": n(ks[1], (BATCH, D), F32),
        "ctx": n(ks[2], (BATCH, CTX_LEN, D), F32),
        "c_ctx": n(ks[3], (D,), F32),
        "w_mod": n(ks[4], (DEPTH, D, N_MOD * D), F32) * (0.5 * D ** -0.5),
        "b_mod": n(ks[5], (DEPTH, N_MOD * D), F32) * 0.01,
        "norm1": 1.0 + 0.02 * n(ks[6], (DEPTH, D), F32),
        "norm2": 1.0 + 0.02 * n(ks[7], (DEPTH, D), F32),
        "w_in": n(ks[8], (DEPTH, D, IN_COLS), F32) * D ** -0.5,
        "b_gate": n(ks[9], (DEPTH, N_BRANCH, D), F32) * 0.1,
        "a_sink": n(ks[10], (DEPTH, A_HEADS), F32) * 0.5,
        "nb_bias": n(ks[11], (DEPTH, B_HEADS, 2 * NB_ROWS - 1, 2 * NB_COLS - 1), F32) * 0.2,
        "c_conv_w": n(ks[12], (DEPTH, C_CONV, C_WIDTH), F32) * C_CONV ** -0.5,
        "c_conv_b": n(ks[13], (DEPTH, C_WIDTH), F32) * 0.01,
        "c_w_a": n(ks[14], (DEPTH, 2, C_BLOCKS, C_BLOCK_W, C_BLOCK_W), F32) * C_BLOCK_W ** -0.5,
        "c_b_a": n(ks[15], (DEPTH, 2, C_WIDTH), F32) * 0.1,
        "c_w_x": n(ks[17], (DEPTH, 2, C_BLOCKS, C_BLOCK_W, C_BLOCK_W), F32) * C_BLOCK_W ** -0.5,
        "c_b_x": n(ks[18], (DEPTH, 2, C_WIDTH), F32) * 0.1,
        "c_lam": jnp.log(lam_s) - jnp.log1p(-lam_s),
        "d_conv_w": n(ks[19], (DEPTH, D_CONV, D_WIDTH), F32) * D_CONV ** -0.5,
        "w_branch": n(ks[20], (DEPTH, N_BRANCH, BRANCH_W, D), F32) * BRANCH_W ** -0.5,
        "w_out": n(ks[21], (DEPTH, D, D), F32) * D ** -0.5,
        "w_ffn_in": n(ks[22], (DEPTH, D, 2 * FFN_HIDDEN), F32) * D ** -0.5,
        "w_ffn_out": n(ks[23], (DEPTH, FFN_HIDDEN, D), F32) * FFN_HIDDEN ** -0.5,
        "final_norm": 1.0 + 0.02 * n(ks[24], (D,), F32),
    }


def reference(x, c, ctx, c_ctx, w_mod, b_mod, norm1, norm2, w_in, b_gate, a_sink, nb_bias,
              c_conv_w, c_conv_b, c_w_a, c_b_a, c_w_x, c_b_x, c_lam, d_conv_w, w_branch, w_out,
              w_ffn_in, w_ffn_out, final_norm):
    seq = x.shape[1]
    t = jnp.arange(seq)
    row, col = t // GRID_W, t % GRID_W
    xc = ctx
    s_lat = jax.nn.silu(c)
    s_ctx = jax.nn.silu(c_ctx)
    for l in range(DEPTH):
        with_ctx = l < DEPTH - 1
        mod = jnp.split((s_lat @ w_mod[l] + b_mod[l])[:, None, :], N_MOD, axis=-1)
        modc = jnp.split((s_ctx @ w_mod[l] + b_mod[l])[None, None, :], N_MOD, axis=-1)
        h = _modulate(_rmsnorm(x, norm1[l]), mod[0], mod[1])
        hc = _modulate(_rmsnorm(xc, norm1[l]), modc[0], modc[1])
        y, yc = _token_mixers(h, hc, row, col, w_in[l], b_gate[l], a_sink[l], nb_bias[l],
                              c_conv_w[l], c_conv_b[l], c_w_a[l], c_b_a[l], c_w_x[l], c_b_x[l],
                              c_lam[l], d_conv_w[l], w_branch[l], w_out[l], with_ctx)
        x = x + mod[2] * y
        x = x + mod[5] * _swiglu(_modulate(_rmsnorm(x, norm2[l]), mod[3], mod[4]), w_ffn_in[l], w_ffn_out[l])
        if with_ctx:
            xc = xc + modc[2] * yc
            xc = xc + modc[5] * _swiglu(_modulate(_rmsnorm(xc, norm2[l]), modc[3], modc[4]),
                                        w_ffn_in[l], w_ffn_out[l])
    return _rmsnorm(x, final_norm)
```

```python
import functools

import numpy as np
import jax
import jax.numpy as jnp
from jax import lax
from jax.experimental import pallas as pl
from jax.experimental.pallas import tpu as pltpu

F32 = jnp.float32
BF16 = jnp.bfloat16

D_MODEL = 2048
DEPTH = 4
GRID_W = 64
EPS = 1e-6
N_MOD = 6
N_BRANCH = 4
BRANCH_W = 512
A_HEADS, A_KV_HEADS, A_HEAD_DIM, A_WINDOW, A_BLOCK = 8, 2, 64, 128, 128
ROPE_THETA = 10000.0
B_HEADS, B_HEAD_DIM, NB_ROWS, NB_COLS = 8, 64, 8, 16
C_BLOCKS, C_BLOCK_W, C_CONV, C_POW = 4, 128, 4, 8.0
D_CONV = 3
FFN_HIDDEN = 5632
A_COLS = (A_HEADS + 2 * A_KV_HEADS) * A_HEAD_DIM
A_ROPE_COLS = (A_HEADS + A_KV_HEADS) * A_HEAD_DIM
B_COLS = 3 * BRANCH_W
CD_COLS = 5 * BRANCH_W
GATE_COL0 = A_COLS + B_COLS + CD_COLS

LANES = 128
SUBLANES = 8
BF16_ROWS = 16
VMEM_BYTES = 64 * 1024 * 1024
VMEM_CAP = VMEM_BYTES - 8 * 1024 * 1024

ROW_CHUNK = 256


def _pick_tile(total, target, mult=BF16_ROWS):
    best = None
    for d in range(mult, min(total, target) + 1, mult):
        if total % d == 0:
            best = d
    assert best is not None, (total, target, mult)
    return best


def _params(semantics, block_bytes, extra_bytes=0):
    need = 2 * sum(block_bytes) + extra_bytes + (4 << 20)
    return pltpu.CompilerParams(dimension_semantics=semantics,
                                vmem_limit_bytes=int(min(max(need, 16 << 20), VMEM_CAP)))


def _nbytes(shape, dtype):
    return int(np.prod(shape)) * jnp.dtype(dtype).itemsize


def _mod_kernel(cs_ref, w_ref, b_ref, o_ref):
    s = jax.nn.silu(cs_ref[...]).astype(BF16)
    w = w_ref[0].astype(BF16)
    o_ref[0] = jnp.dot(s, w, preferred_element_type=F32) + b_ref[0]


def _modulation(cs, w_mod, b_mod):
    depth, d, n = w_mod.shape
    tn = 1024
    return pl.pallas_call(
        _mod_kernel,
        out_shape=jax.ShapeDtypeStruct((depth, SUBLANES, n), F32),
        grid=(depth, n // tn),
        in_specs=[pl.BlockSpec((SUBLANES, d), lambda l, j: (0, 0)),
                  pl.BlockSpec((1, d, tn), lambda l, j: (l, 0, j)),
                  pl.BlockSpec((1, 1, tn), lambda l, j: (l, 0, j))],
        out_specs=pl.BlockSpec((1, SUBLANES, tn), lambda l, j: (l, 0, j)),
        compiler_params=_params(("arbitrary", "arbitrary"), [_nbytes((d, tn), F32)],
                                _nbytes((d, tn), BF16)),
        name="modulation",
    )(cs, w_mod, b_mod.reshape(depth, 1, n))


def _norm_mod_kernel(x_ref, g_ref, mod_ref, o_ref, *, shift_idx, n_lat_blocks):
    x = x_ref[...]
    y = x * lax.rsqrt(jnp.mean(x * x, axis=-1, keepdims=True) + EPS)
    y = y * g_ref[...]
    is_ctx = pl.program_id(0) >= n_lat_blocks
    shift2 = mod_ref[shift_idx]
    scale2 = mod_ref[shift_idx + 1]
    shift = jnp.where(is_ctx, shift2[1:2], shift2[0:1])
    scale = jnp.where(is_ctx, scale2[1:2], scale2[0:1])
    o_ref[...] = (y * (1 + scale) + shift).astype(o_ref.dtype)


def _norm_mod(x, g, mod, shift_idx, n_lat_rows):
    t, d = x.shape
    return pl.pallas_call(
        functools.partial(_norm_mod_kernel, shift_idx=shift_idx,
                          n_lat_blocks=n_lat_rows // ROW_CHUNK),
        out_shape=jax.ShapeDtypeStruct((t, d), BF16),
        grid=(t // ROW_CHUNK,),
        in_specs=[pl.BlockSpec((ROW_CHUNK, d), lambda i: (i, 0)),
                  pl.BlockSpec((1, d), lambda i: (0, 0)),
                  pl.BlockSpec((N_MOD, 2, d), lambda i: (0, 0, 0))],
        out_specs=pl.BlockSpec((ROW_CHUNK, d), lambda i: (i, 0)),
        compiler_params=_params(("arbitrary",), [_nbytes((ROW_CHUNK, d), F32)] * 2,
                                4 * _nbytes((ROW_CHUNK, d), F32)),
        name="norm_mod",
    )(x, g.reshape(1, d), mod)


def _final_norm_kernel(x_ref, g_ref, o_ref):
    x = x_ref[...]
    y = x * lax.rsqrt(jnp.mean(x * x, axis=-1, keepdims=True) + EPS)
    o_ref[...] = y * g_ref[...]


def _final_norm(x, g, n_rows):
    d = x.shape[1]
    return pl.pallas_call(
        _final_norm_kernel,
        out_shape=jax.ShapeDtypeStruct((n_rows, d), F32),
        grid=(n_rows // ROW_CHUNK,),
        in_specs=[pl.BlockSpec((ROW_CHUNK, d), lambda i: (i, 0)),
                  pl.BlockSpec((1, d), lambda i: (0, 0))],
        out_specs=pl.BlockSpec((ROW_CHUNK, d), lambda i: (i, 0)),
        compiler_params=_params(("arbitrary",), [_nbytes((ROW_CHUNK, d), F32)] * 2,
                                2 * _nbytes((ROW_CHUNK, d), F32)),
        name="final_norm",
    )(x, g.reshape(1, d))


def _rope_rotate_half(x):
    lane = lax.broadcasted_iota(jnp.int32, x.shape, 1)
    first = (lane & 16) == 0
    return jnp.where(first, pltpu.roll(x, LANES - 16, 1), pltpu.roll(x, 16, 1))


def _mm_cast_kernel(a_ref, w_ref, o_ref):
    o_ref[...] = jnp.dot(a_ref[...], w_ref[...], preferred_element_type=F32).astype(o_ref.dtype)


def _mm_rope_kernel(a_ref, w_ref, cos_ref, sin_ref, o_ref):
    acc = jnp.dot(a_ref[...], w_ref[...], preferred_element_type=F32)
    cos = cos_ref[...]
    sin = sin_ref[...]
    for c in range(A_ROPE_COLS // LANES):
        xc = acc[:, c * LANES:(c + 1) * LANES]
        o_ref[:, c * LANES:(c + 1) * LANES] = (xc * cos + _rope_rotate_half(xc) * sin).astype(o_ref.dtype)
    o_ref[:, A_ROPE_COLS:] = acc[:, A_ROPE_COLS:].astype(o_ref.dtype)


def _mm_resid_kernel(a_ref, w_ref, x_ref, gate_ref, o_ref, *, n_lat_rows):
    acc = jnp.dot(a_ref[...], w_ref[...], preferred_element_type=F32)
    tm = acc.shape[0]
    rows = pl.program_id(0) * tm + lax.broadcasted_iota(jnp.int32, (tm, 1), 0)
    g2 = gate_ref[0]
    gate = jnp.where(rows >= n_lat_rows, g2[1:2], g2[0:1])
    o_ref[...] = x_ref[...] + gate * acc


def _matmul(a, w, layer, *, tm, tn, out_dtype, name, rope=None, resid=None):
    t, k = a.shape
    n = w.shape[-1]
    grid = (t // tm, n // tn)
    in_specs = [pl.BlockSpec((tm, k), lambda i, j: (i, 0)),
                pl.BlockSpec((None, k, tn), lambda i, j: (layer, 0, j))]
    operands = [a, w]
    blocks = [_nbytes((tm, k), BF16), _nbytes((k, tn), BF16), _nbytes((tm, tn), out_dtype)]
    if rope is not None:
        kernel = _mm_rope_kernel
        in_specs += [pl.BlockSpec((tm, LANES), lambda i, j: (i, 0))] * 2
        operands += list(rope)
    elif resid is not None:
        x, mod, idx, n_lat_rows = resid
        kernel = functools.partial(_mm_resid_kernel, n_lat_rows=n_lat_rows)
        in_specs += [pl.BlockSpec((tm, tn), lambda i, j: (i, j)),
                     pl.BlockSpec((1, 2, tn), lambda i, j: (idx, 0, j))]
        operands += [x, mod]
        blocks.append(_nbytes((tm, tn), F32))
    else:
        kernel = _mm_cast_kernel
    return pl.pallas_call(
        kernel,
        out_shape=jax.ShapeDtypeStruct((t, n), out_dtype),
        grid=grid,
        in_specs=in_specs,
        out_specs=pl.BlockSpec((tm, tn), lambda i, j: (i, j)),
        compiler_params=_params(("arbitrary", "arbitrary"), blocks, 3 * _nbytes((tm, tn), F32)),
        name=name,
    )(*operands)


def _softmax_two(s_loc, s_ctx, sink=None):
    m = jnp.maximum(jnp.max(s_loc, axis=-1, keepdims=True), jnp.max(s_ctx, axis=-1, keepdims=True))
    if sink is not None:
        m = jnp.maximum(m, sink)
    e_loc = jnp.exp(s_loc - m)
    e_ctx = jnp.exp(s_ctx - m)
    den = jnp.sum(e_loc, axis=-1, keepdims=True) + jnp.sum(e_ctx, axis=-1, keepdims=True)
    if sink is not None:
        den = den + jnp.exp(sink - m)
    inv = 1.0 / den
    return e_loc * inv, e_ctx * inv


def _dot_nt(a, b):
    return lax.dot_general(a, b, (((1,), (1,)), ((), ())), preferred_element_type=F32)


def _attn_a_kernel(sink_ref, q_ref, k_ref, v_ref, o_ref, *, n_lat_rows, n_ctx_rows):
    n = pl.program_id(0)
    band = 3 * A_BLOCK
    is_lat = n * A_BLOCK < n_lat_rows
    start = pl.multiple_of(jnp.clip(n * A_BLOCK - A_BLOCK, 0, n_lat_rows - band), A_BLOCK)
    kb = k_ref[pl.ds(start, band), :]
    vb = v_ref[pl.ds(start, band), :]
    kc = k_ref[pl.ds(n_lat_rows, n_ctx_rows), :]
    vc = v_ref[pl.ds(n_lat_rows, n_ctx_rows), :]
    qpos = n * A_BLOCK + lax.broadcasted_iota(jnp.int32, (A_BLOCK, band), 0)
    kpos = start + lax.broadcasted_iota(jnp.int32, (A_BLOCK, band), 1)
    mask = (jnp.abs(qpos - kpos) <= A_WINDOW) & is_lat
    scale = A_HEAD_DIM ** -0.5
    grp = A_HEADS // A_KV_HEADS
    q = q_ref[...]
    outs = []
    for h in range(A_HEADS):
        kv = slice((h // grp) * A_HEAD_DIM, (h // grp + 1) * A_HEAD_DIM)
        qh = q[:, h * A_HEAD_DIM:(h + 1) * A_HEAD_DIM]
        s_loc = jnp.where(mask, _dot_nt(qh, kb[:, kv]) * scale, -jnp.inf)
        s_ctx = _dot_nt(qh, kc[:, kv]) * scale
        p_loc, p_ctx = _softmax_two(s_loc, s_ctx, sink_ref[h])
        outs.append(jnp.dot(p_loc.astype(BF16), vb[:, kv], preferred_element_type=F32)
                    + jnp.dot(p_ctx.astype(BF16), vc[:, kv], preferred_element_type=F32))
    o_ref[...] = jnp.concatenate(outs, axis=1).astype(o_ref.dtype)


def _attn_a(za, sink, n_lat_rows):
    t = za.shape[0]
    qw = A_HEADS * A_HEAD_DIM
    kvw = A_KV_HEADS * A_HEAD_DIM
    return pl.pallas_call(
        functools.partial(_attn_a_kernel, n_lat_rows=n_lat_rows, n_ctx_rows=t - n_lat_rows),
        out_shape=jax.ShapeDtypeStruct((t, qw), BF16),
        grid=(t // A_BLOCK,),
        in_specs=[pl.BlockSpec(memory_space=pltpu.SMEM),
                  pl.BlockSpec((A_BLOCK, qw), lambda n: (n, 0)),
                  pl.BlockSpec((t, kvw), lambda n: (0, qw // kvw)),
                  pl.BlockSpec((t, kvw), lambda n: (0, qw // kvw + 1))],
        out_specs=pl.BlockSpec((A_BLOCK, qw), lambda n: (n, 0)),
        compiler_params=_params(("arbitrary",), [_nbytes((t, kvw), BF16)] * 2, 8 << 20),
        name="attn_a",
    )(sink, za, za, za)


def _nb_bias_tables(rel_bias):
    cq = jnp.arange(GRID_W)
    cstart = jnp.clip(cq - NB_COLS // 2, 0, GRID_W - NB_COLS)
    col_ok = (cq[None, :] >= cstart[:, None]) & (cq[None, :] < cstart[:, None] + NB_COLS)
    dc = jnp.clip(cq[None, :] - cq[:, None] + NB_COLS - 1, 0, 2 * NB_COLS - 2)
    s = jnp.arange(NB_ROWS)
    dr = jnp.arange(NB_ROWS)[None, :] - s[:, None] + NB_ROWS - 1
    tab = rel_bias.astype(F32)[:, dr[:, None, :, None], dc[None, :, None, :]]
    tab = jnp.where(col_ok[None, None, :, None, :], tab, -jnp.inf)
    tab = tab.transpose(1, 0, 2, 3, 4).reshape(NB_ROWS, B_HEADS, GRID_W, NB_ROWS * GRID_W)
    none = jnp.full((1,) + tab.shape[1:], -jnp.inf, F32)
    return jnp.concatenate([tab, none], axis=0)


def _attn_b_kernel(q_ref, k_ref, v_ref, bias_ref, o_ref, *, n_lat_rows, n_ctx_rows):
    r = pl.program_id(0)
    grid_rows = n_lat_rows // GRID_W
    win = NB_ROWS * GRID_W
    start = pl.multiple_of(jnp.clip(r - NB_ROWS // 2, 0, grid_rows - NB_ROWS) * GRID_W, GRID_W)
    kb = k_ref[pl.ds(start, win), :]
    vb = v_ref[pl.ds(start, win), :]
    kc = k_ref[pl.ds(n_lat_rows, n_ctx_rows), :]
    vc = v_ref[pl.ds(n_lat_rows, n_ctx_rows), :]
    scale = B_HEAD_DIM ** -0.5
    q = q_ref[...]
    outs = []
    for h in range(B_HEADS):
        hs = slice(h * B_HEAD_DIM, (h + 1) * B_HEAD_DIM)
        qh = q[:, hs]
        s_loc = _dot_nt(qh, kb[:, hs]) * scale + bias_ref[0, h]
        s_ctx = _dot_nt(qh, kc[:, hs]) * scale
        p_loc, p_ctx = _softmax_two(s_loc, s_ctx)
        outs.append(jnp.dot(p_loc.astype(BF16), vb[:, hs], preferred_element_type=F32)
                    + jnp.dot(p_ctx.astype(BF16), vc[:, hs], preferred_element_type=F32))
    o_ref[...] = jnp.concatenate(outs, axis=1).astype(o_ref.dtype)


def _attn_b(zb, bias_tab, n_lat_rows):
    t = zb.shape[0]
    w = BRANCH_W
    grid_rows = n_lat_rows // GRID_W
    win = NB_ROWS * GRID_W

    def table_index(r):
        inner = jnp.where(r < NB_ROWS // 2, r,
                          jnp.where(r > grid_rows - NB_ROWS // 2, r - (grid_rows - NB_ROWS), NB_ROWS // 2))
        return jnp.where(r >= grid_rows, NB_ROWS, inner)

    return pl.pallas_call(
        functools.partial(_attn_b_kernel, n_lat_rows=n_lat_rows, n_ctx_rows=t - n_lat_rows),
        out_shape=jax.ShapeDtypeStruct((t, w), BF16),
        grid=(t // GRID_W,),
        in_specs=[pl.BlockSpec((GRID_W, w), lambda r: (r, 0)),
                  pl.BlockSpec((t, w), lambda r: (0, 1)),
                  pl.BlockSpec((t, w), lambda r: (0, 2)),
                  pl.BlockSpec((1, B_HEADS, GRID_W, win), lambda r: (table_index(r), 0, 0, 0))],
        out_specs=pl.BlockSpec((GRID_W, w), lambda r: (r, 0)),
        compiler_params=_params(("arbitrary",),
                                [_nbytes((t, w), BF16)] * 2 + [_nbytes((B_HEADS, GRID_W, win), F32)],
                                8 << 20),
        name="attn_b",
    )(zb, zb, zb, bias_tab)


def _cd_prep_kernel(xr_ref, xr_p, xr_n, xd_ref, xd_p, xd_n, cd_ref, cd_p, cd_n, bd_ref,
                    cw_ref, cb_ref, dw_ref, u_ref, yd_ref, *, n_lat_chunks, n_chunks):
    c = pl.program_id(0)
    has_prev = (c != 0) & (c != n_lat_chunks)
    has_next = (c != n_lat_chunks - 1) & (c != n_chunks - 1)
    ch = xr_ref.shape[0]
    h = SUBLANES

    def halo_cat(cur, prev, nxt):
        return jnp.concatenate([jnp.where(has_prev, prev, 0.0), cur, jnp.where(has_next, nxt, 0.0)], axis=0)

    xr = halo_cat(xr_ref[...], xr_p[...], xr_n[...])
    left = C_CONV // 2
    u = xr[h - left:h - left + ch] * cw_ref[0:1, :]
    for j in range(1, C_CONV):
        u = u + xr[h - left + j:h - left + j + ch] * cw_ref[j:j + 1, :]
    u_ref[...] = u + cb_ref[...]

    pd = halo_cat(cd_ref[...] * xd_ref[...], cd_p[...] * xd_p[...], cd_n[...] * xd_n[...])
    left = D_CONV // 2
    y = pd[h - left:h - left + ch] * dw_ref[0:1, :]
    for j in range(1, D_CONV):
        y = y + pd[h - left + j:h - left + j + ch] * dw_ref[j:j + 1, :]
    yd_ref[...] = (bd_ref[...] * y).astype(yd_ref.dtype)


def _cd_prep(zcd, conv_w, conv_b, d_conv_w, n_lat_rows):
    t = zcd.shape[0]
    w = BRANCH_W
    ch = ROW_CHUNK
    n_chunks = t // ch
    per = ch // SUBLANES
    last = t // SUBLANES - 1

    def cur(col):
        return pl.BlockSpec((ch, w), lambda c: (c, col))

    def prev(col):
        return pl.BlockSpec((SUBLANES, w), lambda c: (jnp.maximum(c * per - 1, 0), col))

    def nxt(col):
        return pl.BlockSpec((SUBLANES, w), lambda c: (jnp.minimum((c + 1) * per, last), col))

    small = lambda rows: pl.BlockSpec((rows, w), lambda c: (0, 0))
    return pl.pallas_call(
        functools.partial(_cd_prep_kernel, n_lat_chunks=n_lat_rows // ch, n_chunks=n_chunks),
        out_shape=(jax.ShapeDtypeStruct((t, w), F32), jax.ShapeDtypeStruct((t, w), BF16)),
        grid=(n_chunks,),
        in_specs=[cur(0), prev(0), nxt(0), cur(2), prev(2), nxt(2), cur(4), prev(4), nxt(4), cur(3),
                  small(C_CONV), small(1), small(D_CONV)],
        out_specs=(pl.BlockSpec((ch, w), lambda c: (c, 0)), pl.BlockSpec((ch, w), lambda c: (c, 0))),
        compiler_params=_params(("arbitrary",), [_nbytes((ch, w), F32)] * 6, 8 << 20),
        name="cd_prep",
    )(zcd, zcd, zcd, zcd, zcd, zcd, zcd, zcd, zcd, zcd, conv_w, conv_b.reshape(1, w), d_conv_w)


def _tile_scan(a, b, reverse):
    rows = lax.broadcasted_iota(jnp.int32, a.shape, 0)
    for k in (1, 2, 4):
        if reverse:
            a_sh, b_sh, valid = pltpu.roll(a, SUBLANES - k, 0), pltpu.roll(b, SUBLANES - k, 0), rows < SUBLANES - k
        else:
            a_sh, b_sh, valid = pltpu.roll(a, k, 0), pltpu.roll(b, k, 0), rows >= k
        b = jnp.where(valid, a * b_sh + b, b)
        a = jnp.where(valid, a * a_sh, a)
    return a, b


def _scan_kernel(uf_ref, ub_ref, wg_ref, ba_ref, bx_ref, lam_ref, hf_ref, hb_ref,
                 a_scr, b_scr, carry_f, carry_b):
    @pl.when(pl.program_id(0) == 0)
    def _():
        carry_f[...] = jnp.zeros_like(carry_f)
        carry_b[...] = jnp.zeros_like(carry_b)

    ch, w = uf_ref.shape
    n_tiles = ch // SUBLANES

    def gates(u_ref, d):
        u = u_ref[...]
        ub = u.astype(BF16)
        sp = jax.nn.softplus(-lam_ref[d:d + 1, :])
        for blk in range(C_BLOCKS):
            sl = slice(blk * C_BLOCK_W, (blk + 1) * C_BLOCK_W)
            g = jnp.dot(ub[:, sl], wg_ref[d, blk], preferred_element_type=F32)
            r = jax.nn.sigmoid(g[:, :C_BLOCK_W] + ba_ref[d:d + 1, sl])
            i = jax.nn.sigmoid(g[:, C_BLOCK_W:] + bx_ref[d:d + 1, sl])
            log_a = -C_POW * r * sp[:, sl]
            a = jnp.exp(log_a)
            a_scr[:, sl] = a
            b_scr[:, sl] = jnp.sqrt(-jnp.tanh(log_a) * (a * a + 1.0)) * (i * u[:, sl])

    def scan(out_ref, carry_ref, reverse):
        def body(step, carry):
            tile = n_tiles - 1 - step if reverse else step
            r0 = pl.multiple_of(tile * SUBLANES, SUBLANES)
            a, b = _tile_scan(a_scr[pl.ds(r0, SUBLANES), :], b_scr[pl.ds(r0, SUBLANES), :], reverse)
            h = a * carry + b
            out_ref[pl.ds(r0, SUBLANES), :] = h
            edge = h[0:1, :] if reverse else h[SUBLANES - 1:SUBLANES, :]
            return jnp.broadcast_to(edge, h.shape)

        carry_ref[...] = lax.fori_loop(0, n_tiles, body, carry_ref[...], unroll=4)

    gates(uf_ref, 0)
    scan(hf_ref, carry_f, False)
    gates(ub_ref, 1)
    scan(hb_ref, carry_b, True)


def _rglru_scan(u, wg, b_a, b_x, lam, n_lat_rows):
    t, w = u.shape
    ch = ROW_CHUNK
    n_chunks = t // ch
    n_lat = n_lat_rows // ch
    n_ctx = n_chunks - n_lat

    def fwd_chunk(i):
        return jnp.where(i < n_ctx, n_lat + i, i - n_ctx)

    def bwd_chunk(i):
        return n_chunks - 1 - i

    vec = pl.BlockSpec((2, w), lambda i: (0, 0))
    return pl.pallas_call(
        _scan_kernel,
        out_shape=(jax.ShapeDtypeStruct((t, w), F32), jax.ShapeDtypeStruct((t, w), F32)),
        grid=(n_chunks,),
        in_specs=[pl.BlockSpec((ch, w), lambda i: (fwd_chunk(i), 0)),
                  pl.BlockSpec((ch, w), lambda i: (bwd_chunk(i), 0)),
                  pl.BlockSpec((2, C_BLOCKS, C_BLOCK_W, 2 * C_BLOCK_W), lambda i: (0, 0, 0, 0)),
                  vec, vec, vec],
        out_specs=(pl.BlockSpec((ch, w), lambda i: (fwd_chunk(i), 0)),
                   pl.BlockSpec((ch, w), lambda i: (bwd_chunk(i), 0))),
        scratch_shapes=[pltpu.VMEM((ch, w), F32), pltpu.VMEM((ch, w), F32),
                        pltpu.VMEM((SUBLANES, w), F32), pltpu.VMEM((SUBLANES, w), F32)],
        compiler_params=_params(("arbitrary",), [_nbytes((ch, w), F32)] * 4, 8 << 20),
        name="rglru_scan",
    )(u, u, wg, b_a, b_x, lam)


def _c_final_kernel(hf_ref, hb_ref, gr_ref, o_ref):
    o_ref[...] = ((hf_ref[...] + hb_ref[...]) * jax.nn.gelu(gr_ref[...])).astype(o_ref.dtype)


def _c_final(hf, hb, zcd):
    t, w = hf.shape
    rows = _pick_tile(t, 1024, SUBLANES)
    spec = pl.BlockSpec((rows, w), lambda i: (i, 0))
    return pl.pallas_call(
        _c_final_kernel,
        out_shape=jax.ShapeDtypeStruct((t, w), BF16),
        grid=(t // rows,),
        in_specs=[spec, spec, pl.BlockSpec((rows, w), lambda i: (i, 1))],
        out_specs=spec,
        compiler_params=_params(("arbitrary",), [_nbytes((rows, w), F32)] * 4, 4 << 20),
        name="c_final",
    )(hf, hb, zcd)


def _merge_kernel(h_ref, ya_ref, yb_ref, yr_ref, yd_ref, wg0, wg1, wg2, wg3, wb_ref, bg_ref, o_ref):
    h = h_ref[...]
    merged = None
    for k, (y_ref, wg_ref) in enumerate(zip((ya_ref, yb_ref, yr_ref, yd_ref), (wg0, wg1, wg2, wg3))):
        g = jax.nn.sigmoid(jnp.dot(h, wg_ref[...], preferred_element_type=F32) + bg_ref[k:k + 1, :])
        term = g * jnp.dot(y_ref[...], wb_ref[k], preferred_element_type=F32)
        merged = term if merged is None else merged + term
    o_ref[...] = merged.astype(o_ref.dtype)


def _merge(h, ys, w_gate, w_branch, b_gate, layer, *, tm, tn):
    t, d = h.shape
    per = d // tn
    y_spec = pl.BlockSpec((tm, BRANCH_W), lambda i, j: (i, 0))

    def gate_spec(k):
        return pl.BlockSpec((None, d, tn), lambda i, j: (layer, 0, k * per + j))

    blocks = ([_nbytes((tm, d), BF16)] + [_nbytes((tm, BRANCH_W), BF16)] * N_BRANCH
              + [_nbytes((d, tn), BF16)] * N_BRANCH + [_nbytes((N_BRANCH, BRANCH_W, tn), BF16)]
              + [_nbytes((tm, tn), BF16)])
    return pl.pallas_call(
        _merge_kernel,
        out_shape=jax.ShapeDtypeStruct((t, d), BF16),
        grid=(t // tm, per),
        in_specs=[pl.BlockSpec((tm, d), lambda i, j: (i, 0)), y_spec, y_spec, y_spec, y_spec,
                  gate_spec(0), gate_spec(1), gate_spec(2), gate_spec(3),
                  pl.BlockSpec((None, N_BRANCH, BRANCH_W, tn), lambda i, j: (layer, 0, 0, j)),
                  pl.BlockSpec((None, N_BRANCH, tn), lambda i, j: (layer, 0, j))],
        out_specs=pl.BlockSpec((tm, tn), lambda i, j: (i, j)),
        compiler_params=_params(("arbitrary", "arbitrary"), blocks, 6 * _nbytes((tm, tn), F32)),
        name="merge",
    )(h, *ys, w_gate, w_gate, w_gate, w_gate, w_branch, b_gate)


def _ffn_in_kernel(h_ref, wg_ref, wu_ref, o_ref):
    h = h_ref[...]
    g = jnp.dot(h, wg_ref[...], preferred_element_type=F32)
    u = jnp.dot(h, wu_ref[...], preferred_element_type=F32)
    o_ref[...] = (jax.nn.silu(g) * u).astype(o_ref.dtype)


def _ffn_in(h, w_ffn_in, layer, *, tm, tn):
    t, d = h.shape
    per = FFN_HIDDEN // tn
    blocks = [_nbytes((tm, d), BF16), _nbytes((d, tn), BF16), _nbytes((d, tn), BF16), _nbytes((tm, tn), BF16)]
    return pl.pallas_call(
        _ffn_in_kernel,
        out_shape=jax.ShapeDtypeStruct((t, FFN_HIDDEN), BF16),
        grid=(t // tm, per),
        in_specs=[pl.BlockSpec((tm, d), lambda i, j: (i, 0)),
                  pl.BlockSpec((None, d, tn), lambda i, j: (layer, 0, j)),
                  pl.BlockSpec((None, d, tn), lambda i, j: (layer, 0, per + j))],
        out_specs=pl.BlockSpec((tm, tn), lambda i, j: (i, j)),
        compiler_params=_params(("arbitrary", "arbitrary"), blocks, 4 * _nbytes((tm, tn), F32)),
        name="ffn_in",
    )(h, w_ffn_in, w_ffn_in)


def _rope_tables(n_lat_rows, n_ctx_rows):
    half = A_HEAD_DIM // 4
    t = jnp.arange(n_lat_rows)
    freqs = ROPE_THETA ** (-jnp.arange(half, dtype=F32) / half)

    def cos_sin(pos):
        ang = pos.astype(F32)[:, None] * freqs[None, :]
        return jnp.cos(ang), jnp.sin(ang)

    cr, sr = cos_sin(t // GRID_W)
    cc, sc = cos_sin(t % GRID_W)
    cos = jnp.concatenate([cr, cr, cc, cc], axis=-1)
    sin = jnp.concatenate([-sr, sr, -sc, sc], axis=-1)
    cos = jnp.concatenate([cos, jnp.ones((n_ctx_rows, A_HEAD_DIM), F32)], axis=0)
    sin = jnp.concatenate([sin, jnp.zeros((n_ctx_rows, A_HEAD_DIM), F32)], axis=0)
    reps = LANES // A_HEAD_DIM
    return jnp.tile(cos, (1, reps)), jnp.tile(sin, (1, reps))


def kernel(x, c, ctx, c_ctx, w_mod, b_mod, norm1, norm2, w_in, b_gate, a_sink, nb_bias, c_conv_w, c_conv_b,
           c_w_a, c_b_a, c_w_x, c_b_x, c_lam, d_conv_w, w_branch, w_out, w_ffn_in, w_ffn_out, final_norm):
    bsz, s, d = x.shape
    l_ctx = ctx.shape[1]
    assert bsz == 1 and d == D_MODEL and c.shape[0] == 1
    assert s % ROW_CHUNK == 0 and l_ctx % ROW_CHUNK == 0 and s // GRID_W >= NB_ROWS and s >= 3 * A_BLOCK
    t = s + l_ctx
    depth = w_mod.shape[0]

    w_a_cols = w_in[:, :, :A_COLS].astype(BF16)
    w_b_cols = w_in[:, :, A_COLS:A_COLS + B_COLS].astype(BF16)
    w_cd_cols = w_in[:, :, A_COLS + B_COLS:GATE_COL0].astype(BF16)
    w_gate = w_in[:, :, GATE_COL0:].astype(BF16)
    w_branch_b = w_branch.astype(BF16)
    w_out_b = w_out.astype(BF16)
    w_ffn_in_b = w_ffn_in.astype(BF16)
    w_ffn_out_b = w_ffn_out.astype(BF16)
    wg_scan = jnp.concatenate([c_w_a, c_w_x], axis=-1).astype(BF16)

    cos_t, sin_t = _rope_tables(s, l_ctx)

    cs = jnp.concatenate([c, c_ctx[None, :], jnp.zeros((SUBLANES - 2, d), F32)], axis=0)
    mod_all = _modulation(cs, w_mod, b_mod)
    mod_all = mod_all[:, :2].reshape(depth, 2, N_MOD, d).transpose(0, 2, 1, 3)

    xt = jnp.concatenate([x[0], ctx[0]], axis=0)

    tm_big = _pick_tile(t, 1408)
    tm_small = _pick_tile(t, 704)
    for l in range(depth):
        mod = mod_all[l]
        h = _norm_mod(xt, norm1[l], mod, 0, s)
        za = _matmul(h, w_a_cols, l, tm=tm_big, tn=A_COLS, out_dtype=BF16, name="proj_a", rope=(cos_t, sin_t))
        zb = _matmul(h, w_b_cols, l, tm=tm_big, tn=B_COLS // 2, out_dtype=BF16, name="proj_b")
        zcd = _matmul(h, w_cd_cols, l, tm=tm_big, tn=BRANCH_W, out_dtype=F32, name="proj_cd")
        ya = _attn_a(za, a_sink[l], s)
        yb = _attn_b(zb, _nb_bias_tables(nb_bias[l]), s)
        u, yd = _cd_prep(zcd, c_conv_w[l], c_conv_b[l], d_conv_w[l], s)
        hf, hb = _rglru_scan(u, wg_scan[l], c_b_a[l], c_b_x[l], c_lam[l], s)
        yr = _c_final(hf, hb, zcd)
        merged = _merge(h, (ya, yb, yr, yd), w_gate, w_branch_b, b_gate, l, tm=tm_small, tn=256)
        xt = _matmul(merged, w_out_b, l, tm=tm_big, tn=512, out_dtype=F32, name="proj_out",
                     resid=(xt, mod, 2, s))
        h2 = _norm_mod(xt, norm2[l], mod, 3, s)
        act = _ffn_in(h2, w_ffn_in_b, l, tm=tm_big, tn=512)
        xt = _matmul(act, w_ffn_out_b, l, tm=tm_small, tn=512, out_dtype=F32, name="ffn_out",
                     resid=(xt, mod, 5, s))
    return _final_norm(xt, final_norm, s)[None]
```

```python
import functools

import numpy as np
import jax
import jax.numpy as jnp
from jax import lax
from jax.experimental import pallas as pl
from jax.experimental.pallas import tpu as pltpu

F32 = jnp.float32
BF16 = jnp.bfloat16

D_MODEL = 2048
DEPTH = 4
GRID_W = 64
EPS = 1e-6
N_MOD = 6
N_BRANCH = 4
BRANCH_W = 512
A_HEADS, A_KV_HEADS, A_HEAD_DIM, A_WINDOW, A_BLOCK = 8, 2, 64, 128, 128
ROPE_THETA = 10000.0
B_HEADS, B_HEAD_DIM, NB_ROWS, NB_COLS = 8, 64, 8, 16
C_BLOCKS, C_BLOCK_W, C_CONV, C_POW = 4, 128, 4, 8.0
D_CONV = 3
FFN_HIDDEN = 5632
A_COLS = (A_HEADS + 2 * A_KV_HEADS) * A_HEAD_DIM
A_ROPE_COLS = (A_HEADS + A_KV_HEADS) * A_HEAD_DIM
B_COLS = 3 * BRANCH_W
CD_COLS = 5 * BRANCH_W
GATE_COL0 = A_COLS + B_COLS + CD_COLS

LANES = 128
SUBLANES = 8
BF16_ROWS = 16
VMEM_BYTES = 64 * 1024 * 1024
VMEM_CAP = VMEM_BYTES - 8 * 1024 * 1024

ROW_CHUNK = 256
ATTN_ROWS = 256


def _pick_tile(total, target, mult=BF16_ROWS):
    best = None
    for d in range(mult, min(total, target) + 1, mult):
        if total % d == 0:
            best = d
    assert best is not None, (total, target, mult)
    return best


def _params(semantics, block_bytes, extra_bytes=0):
    need = 2 * sum(block_bytes) + extra_bytes + (4 << 20)
    return pltpu.CompilerParams(dimension_semantics=semantics,
                                vmem_limit_bytes=int(min(max(need, 16 << 20), VMEM_CAP)))


def _nbytes(shape, dtype):
    return int(np.prod(shape)) * jnp.dtype(dtype).itemsize


def _mod_kernel(cs_ref, w_ref, b_ref, o_ref):
    s = jax.nn.silu(cs_ref[...]).astype(BF16)
    w = w_ref[0].astype(BF16)
    o_ref[0] = jnp.dot(s, w, preferred_element_type=F32) + b_ref[0]


def _modulation(cs, w_mod, b_mod):
    depth, d, n = w_mod.shape
    tn = 1024
    return pl.pallas_call(
        _mod_kernel,
        out_shape=jax.ShapeDtypeStruct((depth, SUBLANES, n), F32),
        grid=(depth, n // tn),
        in_specs=[pl.BlockSpec((SUBLANES, d), lambda l, j: (0, 0)),
                  pl.BlockSpec((1, d, tn), lambda l, j: (l, 0, j)),
                  pl.BlockSpec((1, 1, tn), lambda l, j: (l, 0, j))],
        out_specs=pl.BlockSpec((1, SUBLANES, tn), lambda l, j: (l, 0, j)),
        compiler_params=_params(("arbitrary", "arbitrary"), [_nbytes((d, tn), F32)],
                                _nbytes((d, tn), BF16)),
        name="modulation",
    )(cs, w_mod, b_mod.reshape(depth, 1, n))


def _norm_mod_kernel(x_ref, g_ref, mod_ref, o_ref, *, shift_idx, n_lat_blocks):
    x = x_ref[...]
    y = x * lax.rsqrt(jnp.mean(x * x, axis=-1, keepdims=True) + EPS)
    y = y * g_ref[...]
    is_ctx = pl.program_id(0) >= n_lat_blocks
    shift2 = mod_ref[shift_idx]
    scale2 = mod_ref[shift_idx + 1]
    shift = jnp.where(is_ctx, shift2[1:2], shift2[0:1])
    scale = jnp.where(is_ctx, scale2[1:2], scale2[0:1])
    o_ref[...] = (y * (1 + scale) + shift).astype(o_ref.dtype)


def _norm_mod(x, g, mod, shift_idx, n_lat_rows):
    t, d = x.shape
    return pl.pallas_call(
        functools.partial(_norm_mod_kernel, shift_idx=shift_idx,
                          n_lat_blocks=n_lat_rows // ROW_CHUNK),
        out_shape=jax.ShapeDtypeStruct((t, d), BF16),
        grid=(t // ROW_CHUNK,),
        in_specs=[pl.BlockSpec((ROW_CHUNK, d), lambda i: (i, 0)),
                  pl.BlockSpec((1, d), lambda i: (0, 0)),
                  pl.BlockSpec((N_MOD, 2, d), lambda i: (0, 0, 0))],
        out_specs=pl.BlockSpec((ROW_CHUNK, d), lambda i: (i, 0)),
        compiler_params=_params(("arbitrary",), [_nbytes((ROW_CHUNK, d), F32)] * 2,
                                4 * _nbytes((ROW_CHUNK, d), F32)),
        name="norm_mod",
    )(x, g.reshape(1, d), mod)


def _final_norm_kernel(x_ref, g_ref, o_ref):
    x = x_ref[...]
    y = x * lax.rsqrt(jnp.mean(x * x, axis=-1, keepdims=True) + EPS)
    o_ref[...] = y * g_ref[...]


def _final_norm(x, g, n_rows):
    d = x.shape[1]
    return pl.pallas_call(
        _final_norm_kernel,
        out_shape=jax.ShapeDtypeStruct((n_rows, d), F32),
        grid=(n_rows // ROW_CHUNK,),
        in_specs=[pl.BlockSpec((ROW_CHUNK, d), lambda i: (i, 0)),
                  pl.BlockSpec((1, d), lambda i: (0, 0))],
        out_specs=pl.BlockSpec((ROW_CHUNK, d), lambda i: (i, 0)),
        compiler_params=_params(("arbitrary",), [_nbytes((ROW_CHUNK, d), F32)] * 2,
                                2 * _nbytes((ROW_CHUNK, d), F32)),
        name="final_norm",
    )(x, g.reshape(1, d))


def _rope_rotate_half(x):
    lane = lax.broadcasted_iota(jnp.int32, x.shape, 1)
    first = (lane & 16) == 0
    return jnp.where(first, pltpu.roll(x, LANES - 16, 1), pltpu.roll(x, 16, 1))


def _mm_cast_kernel(a_ref, w_ref, o_ref):
    o_ref[...] = jnp.dot(a_ref[...], w_ref[...], preferred_element_type=F32).astype(o_ref.dtype)


def _mm_rope_kernel(a_ref, w_ref, cos_ref, sin_ref, o_ref):
    acc = jnp.dot(a_ref[...], w_ref[...], preferred_element_type=F32)
    cos = cos_ref[...]
    sin = sin_ref[...]
    for c in range(A_ROPE_COLS // LANES):
        xc = acc[:, c * LANES:(c + 1) * LANES]
        o_ref[:, c * LANES:(c + 1) * LANES] = (xc * cos + _rope_rotate_half(xc) * sin).astype(o_ref.dtype)
    o_ref[:, A_ROPE_COLS:] = acc[:, A_ROPE_COLS:].astype(o_ref.dtype)


def _mm_resid_kernel(a_ref, w_ref, x_ref, gate_ref, o_ref, *, n_lat_rows):
    acc = jnp.dot(a_ref[...], w_ref[...], preferred_element_type=F32)
    tm = acc.shape[0]
    rows = pl.program_id(0) * tm + lax.broadcasted_iota(jnp.int32, (tm, 1), 0)
    g2 = gate_ref[0]
    gate = jnp.where(rows >= n_lat_rows, g2[1:2], g2[0:1])
    o_ref[...] = x_ref[...] + gate * acc


def _matmul(a, w, layer, *, tm, tn, out_dtype, name, rope=None, resid=None):
    t, k = a.shape
    n = w.shape[-1]
    grid = (t // tm, n // tn)
    in_specs = [pl.BlockSpec((tm, k), lambda i, j: (i, 0)),
                pl.BlockSpec((None, k, tn), lambda i, j: (layer, 0, j))]
    operands = [a, w]
    blocks = [_nbytes((tm, k), BF16), _nbytes((k, tn), BF16), _nbytes((tm, tn), out_dtype)]
    if rope is not None:
        kernel = _mm_rope_kernel
        in_specs += [pl.BlockSpec((tm, LANES), lambda i, j: (i, 0))] * 2
        operands += list(rope)
    elif resid is not None:
        x, mod, idx, n_lat_rows = resid
        kernel = functools.partial(_mm_resid_kernel, n_lat_rows=n_lat_rows)
        in_specs += [pl.BlockSpec((tm, tn), lambda i, j: (i, j)),
                     pl.BlockSpec((1, 2, tn), lambda i, j: (idx, 0, j))]
        operands += [x, mod]
        blocks.append(_nbytes((tm, tn), F32))
    else:
        kernel = _mm_cast_kernel
    return pl.pallas_call(
        kernel,
        out_shape=jax.ShapeDtypeStruct((t, n), out_dtype),
        grid=grid,
        in_specs=in_specs,
        out_specs=pl.BlockSpec((tm, tn), lambda i, j: (i, j)),
        compiler_params=_params(("arbitrary", "arbitrary"), blocks, 3 * _nbytes((tm, tn), F32)),
        name=name,
    )(*operands)


def _dot_nt(a, b):
    return lax.dot_general(a, b, (((1,), (1,)), ((), ())), preferred_element_type=F32)


def _attend(q, kb, vb, kc, vc, bias, sink=None):
    s_loc = _dot_nt(q, kb) + bias
    s_ctx = _dot_nt(q, kc)
    m = jnp.maximum(jnp.max(s_loc, axis=-1, keepdims=True), jnp.max(s_ctx, axis=-1, keepdims=True))
    if sink is not None:
        m = jnp.maximum(m, sink)
    e_loc = jnp.exp(s_loc - m)
    e_ctx = jnp.exp(s_ctx - m)
    den = jnp.sum(e_loc, axis=-1, keepdims=True) + jnp.sum(e_ctx, axis=-1, keepdims=True)
    if sink is not None:
        den = den + jnp.exp(sink - m)
    o = (jnp.dot(e_loc.astype(BF16), vb, preferred_element_type=F32)
         + jnp.dot(e_ctx.astype(BF16), vc, preferred_element_type=F32))
    return o / den


def _attn_a_kernel(sink_ref, q_ref, k_ref, v_ref, o_ref, *, n_lat_rows, n_ctx_rows):
    n = pl.program_id(0)
    rows = q_ref.shape[0]
    band = rows + 2 * A_WINDOW
    is_lat = n * rows < n_lat_rows
    start = pl.multiple_of(jnp.clip(n * rows - A_WINDOW, 0, n_lat_rows - band), A_BLOCK)
    kb = k_ref[pl.ds(start, band), :]
    vb = v_ref[pl.ds(start, band), :]
    kc = k_ref[pl.ds(n_lat_rows, n_ctx_rows), :]
    vc = v_ref[pl.ds(n_lat_rows, n_ctx_rows), :]
    qpos = n * rows + lax.broadcasted_iota(jnp.int32, (rows, band), 0)
    kpos = start + lax.broadcasted_iota(jnp.int32, (rows, band), 1)
    bias = jnp.where((jnp.abs(qpos - kpos) <= A_WINDOW) & is_lat, 0.0, -jnp.inf)
    scale = A_HEAD_DIM ** -0.5
    grp = A_HEADS // A_KV_HEADS
    q = q_ref[...] * scale
    outs = []
    for h in range(A_HEADS):
        kv = slice((h // grp) * A_HEAD_DIM, (h // grp + 1) * A_HEAD_DIM)
        outs.append(_attend(q[:, h * A_HEAD_DIM:(h + 1) * A_HEAD_DIM],
                            kb[:, kv], vb[:, kv], kc[:, kv], vc[:, kv], bias, sink_ref[h]))
    o_ref[...] = jnp.concatenate(outs, axis=1).astype(o_ref.dtype)


def _attn_a(za, sink, n_lat_rows):
    t = za.shape[0]
    qw = A_HEADS * A_HEAD_DIM
    kvw = A_KV_HEADS * A_HEAD_DIM
    assert A_WINDOW == A_BLOCK and n_lat_rows >= ATTN_ROWS + 2 * A_WINDOW
    resident = pl.Buffered(1)
    return pl.pallas_call(
        functools.partial(_attn_a_kernel, n_lat_rows=n_lat_rows, n_ctx_rows=t - n_lat_rows),
        out_shape=jax.ShapeDtypeStruct((t, qw), BF16),
        grid=(t // ATTN_ROWS,),
        in_specs=[pl.BlockSpec(memory_space=pltpu.SMEM),
                  pl.BlockSpec((ATTN_ROWS, qw), lambda n: (n, 0)),
                  pl.BlockSpec((t, kvw), lambda n: (0, qw // kvw), pipeline_mode=resident),
                  pl.BlockSpec((t, kvw), lambda n: (0, qw // kvw + 1), pipeline_mode=resident)],
        out_specs=pl.BlockSpec((ATTN_ROWS, qw), lambda n: (n, 0)),
        compiler_params=_params(("arbitrary",), [_nbytes((t, kvw), BF16)] * 2, 16 << 20),
        name="attn_a",
    )(sink, za, za, za)


B_GROUP = ATTN_ROWS // GRID_W
B_SPAN = 12
B_TABLE_FIRST, B_TABLE_INNER, B_TABLE_LAST, B_TABLE_CTX = range(4)


def _nb_bias_tables(rel_bias):
    assert B_GROUP == NB_ROWS // 2 and B_SPAN >= NB_ROWS + B_GROUP - 1 and (B_SPAN * GRID_W) % LANES == 0
    cq = np.arange(GRID_W)
    cstart = np.clip(cq - NB_COLS // 2, 0, GRID_W - NB_COLS)
    col_ok = (cq[None, :] >= cstart[:, None]) & (cq[None, :] < cstart[:, None] + NB_COLS)
    b = rel_bias.astype(F32)
    pad = GRID_W - NB_COLS
    bpad = jnp.concatenate([jnp.repeat(b[..., :1], pad, axis=-1), b,
                            jnp.repeat(b[..., -1:], pad, axis=-1)], axis=-1)
    off = pad + NB_COLS - 1
    toep = jnp.stack([bpad[..., off - q:off - q + GRID_W] for q in range(GRID_W)], axis=-2)
    toep = jnp.where(col_ok, toep, -jnp.inf)
    hidden = jnp.full_like(toep[..., 0, :, :], -jnp.inf)
    layouts = {B_TABLE_FIRST: (lambda j: 0, NB_ROWS - 1),
               B_TABLE_INNER: (lambda j: j, NB_ROWS - 1 - NB_ROWS // 2),
               B_TABLE_LAST: (lambda j: B_SPAN - NB_ROWS, NB_ROWS - 1 + B_GROUP - B_SPAN)}
    tabs = []
    for kind in (B_TABLE_FIRST, B_TABLE_INNER, B_TABLE_LAST):
        first_row, dr0 = layouts[kind]
        rows = []
        for j in range(B_GROUP):
            blocks = [toep[..., lj - j + dr0, :, :] if first_row(j) <= lj < first_row(j) + NB_ROWS else hidden
                      for lj in range(B_SPAN)]
            rows.append(jnp.concatenate(blocks, axis=-1))
        tabs.append(jnp.concatenate(rows, axis=-2))
    tabs.append(jnp.full_like(tabs[0], -jnp.inf))
    return jnp.stack(tabs, axis=-4)


def _attn_b_kernel(q_ref, k_ref, v_ref, bias_ref, o_ref, *, n_lat_rows, n_ctx_rows):
    g = pl.program_id(0)
    grid_rows = n_lat_rows // GRID_W
    win = B_SPAN * GRID_W
    first = jnp.clip(g * B_GROUP - NB_ROWS // 2, 0, grid_rows - B_SPAN)
    start = pl.multiple_of(first * GRID_W, GRID_W)
    kb = k_ref[pl.ds(start, win), :]
    vb = v_ref[pl.ds(start, win), :]
    kc = k_ref[pl.ds(n_lat_rows, n_ctx_rows), :]
    vc = v_ref[pl.ds(n_lat_rows, n_ctx_rows), :]
    q = q_ref[...] * (B_HEAD_DIM ** -0.5)
    outs = []
    for h in range(B_HEADS):
        hs = slice(h * B_HEAD_DIM, (h + 1) * B_HEAD_DIM)
        outs.append(_attend(q[:, hs], kb[:, hs], vb[:, hs], kc[:, hs], vc[:, hs], bias_ref[0, h]))
    o_ref[...] = jnp.concatenate(outs, axis=1).astype(o_ref.dtype)


def _attn_b(zb, bias_tabs, layer, n_lat_rows):
    t = zb.shape[0]
    w = BRANCH_W
    n_lat_groups = n_lat_rows // ATTN_ROWS
    win = B_SPAN * GRID_W
    assert n_lat_rows // GRID_W >= B_SPAN and n_lat_groups >= 2

    def table_index(g):
        inner = jnp.where(g == 0, B_TABLE_FIRST, jnp.where(g == n_lat_groups - 1, B_TABLE_LAST, B_TABLE_INNER))
        return jnp.where(g >= n_lat_groups, B_TABLE_CTX, inner)

    resident = pl.Buffered(1)
    table_block = (None, 1, B_HEADS, ATTN_ROWS, win)
    return pl.pallas_call(
        functools.partial(_attn_b_kernel, n_lat_rows=n_lat_rows, n_ctx_rows=t - n_lat_rows),
        out_shape=jax.ShapeDtypeStruct((t, w), BF16),
        grid=(t // ATTN_ROWS,),
        in_specs=[pl.BlockSpec((ATTN_ROWS, w), lambda g: (g, 0)),
                  pl.BlockSpec((t, w), lambda g: (0, 1), pipeline_mode=resident),
                  pl.BlockSpec((t, w), lambda g: (0, 2), pipeline_mode=resident),
                  pl.BlockSpec(table_block, lambda g: (layer, table_index(g), 0, 0, 0))],
        out_specs=pl.BlockSpec((ATTN_ROWS, w), lambda g: (g, 0)),
        compiler_params=_params(("arbitrary",),
                                [_nbytes((t, w), BF16), _nbytes((B_HEADS, ATTN_ROWS, win), F32)],
                                16 << 20),
        name="attn_b",
    )(zb, zb, zb, bias_tabs)


def _cd_prep_kernel(xr_ref, xr_p, xr_n, xd_ref, xd_p, xd_n, cd_ref, cd_p, cd_n, bd_ref,
                    cw_ref, cb_ref, dw_ref, u_ref, yd_ref, *, n_lat_chunks, n_chunks):
    c = pl.program_id(0)
    has_prev = (c != 0) & (c != n_lat_chunks)
    has_next = (c != n_lat_chunks - 1) & (c != n_chunks - 1)
    ch = xr_ref.shape[0]
    h = SUBLANES

    def halo_cat(cur, prev, nxt):
        return jnp.concatenate([jnp.where(has_prev, prev, 0.0), cur, jnp.where(has_next, nxt, 0.0)], axis=0)

    xr = halo_cat(xr_ref[...], xr_p[...], xr_n[...])
    left = C_CONV // 2
    u = xr[h - left:h - left + ch] * cw_ref[0:1, :]
    for j in range(1, C_CONV):
        u = u + xr[h - left + j:h - left + j + ch] * cw_ref[j:j + 1, :]
    u_ref[...] = u + cb_ref[...]

    pd = halo_cat(cd_ref[...] * xd_ref[...], cd_p[...] * xd_p[...], cd_n[...] * xd_n[...])
    left = D_CONV // 2
    y = pd[h - left:h - left + ch] * dw_ref[0:1, :]
    for j in range(1, D_CONV):
        y = y + pd[h - left + j:h - left + j + ch] * dw_ref[j:j + 1, :]
    yd_ref[...] = (bd_ref[...] * y).astype(yd_ref.dtype)


def _cd_prep(zcd, conv_w, conv_b, d_conv_w, n_lat_rows):
    t = zcd.shape[0]
    w = BRANCH_W
    ch = ROW_CHUNK
    n_chunks = t // ch
    per = ch // SUBLANES
    last = t // SUBLANES - 1

    def cur(col):
        return pl.BlockSpec((ch, w), lambda c: (c, col))

    def prev(col):
        return pl.BlockSpec((SUBLANES, w), lambda c: (jnp.maximum(c * per - 1, 0), col))

    def nxt(col):
        return pl.BlockSpec((SUBLANES, w), lambda c: (jnp.minimum((c + 1) * per, last), col))

    small = lambda rows: pl.BlockSpec((rows, w), lambda c: (0, 0))
    return pl.pallas_call(
        functools.partial(_cd_prep_kernel, n_lat_chunks=n_lat_rows // ch, n_chunks=n_chunks),
        out_shape=(jax.ShapeDtypeStruct((t, w), F32), jax.ShapeDtypeStruct((t, w), BF16)),
        grid=(n_chunks,),
        in_specs=[cur(0), prev(0), nxt(0), cur(2), prev(2), nxt(2), cur(4), prev(4), nxt(4), cur(3),
                  small(C_CONV), small(1), small(D_CONV)],
        out_specs=(pl.BlockSpec((ch, w), lambda c: (c, 0)), pl.BlockSpec((ch, w), lambda c: (c, 0))),
        compiler_params=_params(("arbitrary",), [_nbytes((ch, w), F32)] * 6, 8 << 20),
        name="cd_prep",
    )(zcd, zcd, zcd, zcd, zcd, zcd, zcd, zcd, zcd, zcd, conv_w, conv_b.reshape(1, w), d_conv_w)


def _tile_scan(a, b, reverse):
    rows = lax.broadcasted_iota(jnp.int32, a.shape, 0)
    for k in (1, 2, 4):
        if reverse:
            a_sh, b_sh, valid = pltpu.roll(a, SUBLANES - k, 0), pltpu.roll(b, SUBLANES - k, 0), rows < SUBLANES - k
        else:
            a_sh, b_sh, valid = pltpu.roll(a, k, 0), pltpu.roll(b, k, 0), rows >= k
        b = jnp.where(valid, a * b_sh + b, b)
        a = jnp.where(valid, a * a_sh, a)
    return a, b


def _scan_kernel(uf_ref, ub_ref, wg_ref, ba_ref, bx_ref, lam_ref, hf_ref, hb_ref,
                 a_scr, b_scr, carry_f, carry_b):
    @pl.when(pl.program_id(0) == 0)
    def _():
        carry_f[...] = jnp.zeros_like(carry_f)
        carry_b[...] = jnp.zeros_like(carry_b)

    ch, w = uf_ref.shape
    n_tiles = ch // SUBLANES

    def gates(u_ref, d):
        u = u_ref[...]
        ub = u.astype(BF16)
        sp = jax.nn.softplus(-lam_ref[d:d + 1, :])
        for blk in range(C_BLOCKS):
            sl = slice(blk * C_BLOCK_W, (blk + 1) * C_BLOCK_W)
            g = jnp.dot(ub[:, sl], wg_ref[d, blk], preferred_element_type=F32)
            r = jax.nn.sigmoid(g[:, :C_BLOCK_W] + ba_ref[d:d + 1, sl])
            i = jax.nn.sigmoid(g[:, C_BLOCK_W:] + bx_ref[d:d + 1, sl])
            log_a = -C_POW * r * sp[:, sl]
            a = jnp.exp(log_a)
            a_scr[:, sl] = a
            b_scr[:, sl] = jnp.sqrt(-jnp.tanh(log_a) * (a * a + 1.0)) * (i * u[:, sl])

    def scan(out_ref, carry_ref, reverse):
        def body(step, carry):
            tile = n_tiles - 1 - step if reverse else step
            r0 = pl.multiple_of(tile * SUBLANES, SUBLANES)
            a, b = _tile_scan(a_scr[pl.ds(r0, SUBLANES), :], b_scr[pl.ds(r0, SUBLANES), :], reverse)
            h = a * carry + b
            out_ref[pl.ds(r0, SUBLANES), :] = h
            edge = h[0:1, :] if reverse else h[SUBLANES - 1:SUBLANES, :]
            return jnp.broadcast_to(edge, h.shape)

        carry_ref[...] = lax.fori_loop(0, n_tiles, body, carry_ref[...], unroll=4)

    gates(uf_ref, 0)
    scan(hf_ref, carry_f, False)
    gates(ub_ref, 1)
    scan(hb_ref, carry_b, True)


def _rglru_scan(u, wg, b_a, b_x, lam, n_lat_rows):
    t, w = u.shape
    ch = ROW_CHUNK
    n_chunks = t // ch
    n_lat = n_lat_rows // ch
    n_ctx = n_chunks - n_lat

    def fwd_chunk(i):
        return jnp.where(i < n_ctx, n_lat + i, i - n_ctx)

    def bwd_chunk(i):
        return n_chunks - 1 - i

    vec = pl.BlockSpec((2, w), lambda i: (0, 0))
    return pl.pallas_call(
        _scan_kernel,
        out_shape=(jax.ShapeDtypeStruct((t, w), F32), jax.ShapeDtypeStruct((t, w), F32)),
        grid=(n_chunks,),
        in_specs=[pl.BlockSpec((ch, w), lambda i: (fwd_chunk(i), 0)),
                  pl.BlockSpec((ch, w), lambda i: (bwd_chunk(i), 0)),
                  pl.BlockSpec((2, C_BLOCKS, C_BLOCK_W, 2 * C_BLOCK_W), lambda i: (0, 0, 0, 0)),
                  vec, vec, vec],
        out_specs=(pl.BlockSpec((ch, w), lambda i: (fwd_chunk(i), 0)),
                   pl.BlockSpec((ch, w), lambda i: (bwd_chunk(i), 0))),
        scratch_shapes=[pltpu.VMEM((ch, w), F32), pltpu.VMEM((ch, w), F32),
                        pltpu.VMEM((SUBLANES, w), F32), pltpu.VMEM((SUBLANES, w), F32)],
        compiler_params=_params(("arbitrary",), [_nbytes((ch, w), F32)] * 4, 8 << 20),
        name="rglru_scan",
    )(u, u, wg, b_a, b_x, lam)


def _c_final_kernel(hf_ref, hb_ref, gr_ref, o_ref):
    o_ref[...] = ((hf_ref[...] + hb_ref[...]) * jax.nn.gelu(gr_ref[...])).astype(o_ref.dtype)


def _c_final(hf, hb, zcd):
    t, w = hf.shape
    rows = _pick_tile(t, 1024, SUBLANES)
    spec = pl.BlockSpec((rows, w), lambda i: (i, 0))
    return pl.pallas_call(
        _c_final_kernel,
        out_shape=jax.ShapeDtypeStruct((t, w), BF16),
        grid=(t // rows,),
        in_specs=[spec, spec, pl.BlockSpec((rows, w), lambda i: (i, 1))],
        out_specs=spec,
        compiler_params=_params(("arbitrary",), [_nbytes((rows, w), F32)] * 4, 4 << 20),
        name="c_final",
    )(hf, hb, zcd)


def _merge_kernel(h_ref, ya_ref, yb_ref, yr_ref, yd_ref, wg0, wg1, wg2, wg3, wb_ref, bg_ref, o_ref):
    h = h_ref[...]
    merged = None
    for k, (y_ref, wg_ref) in enumerate(zip((ya_ref, yb_ref, yr_ref, yd_ref), (wg0, wg1, wg2, wg3))):
        g = jax.nn.sigmoid(jnp.dot(h, wg_ref[...], preferred_element_type=F32) + bg_ref[k:k + 1, :])
        term = g * jnp.dot(y_ref[...], wb_ref[k], preferred_element_type=F32)
        merged = term if merged is None else merged + term
    o_ref[...] = merged.astype(o_ref.dtype)


def _merge(h, ys, w_gate, w_branch, b_gate, layer, *, tm, tn):
    t, d = h.shape
    per = d // tn
    y_spec = pl.BlockSpec((tm, BRANCH_W), lambda i, j: (i, 0))

    def gate_spec(k):
        return pl.BlockSpec((None, d, tn), lambda i, j: (layer, 0, k * per + j))

    blocks = ([_nbytes((tm, d), BF16)] + [_nbytes((tm, BRANCH_W), BF16)] * N_BRANCH
              + [_nbytes((d, tn), BF16)] * N_BRANCH + [_nbytes((N_BRANCH, BRANCH_W, tn), BF16)]
              + [_nbytes((tm, tn), BF16)])
    return pl.pallas_call(
        _merge_kernel,
        out_shape=jax.ShapeDtypeStruct((t, d), BF16),
        grid=(t // tm, per),
        in_specs=[pl.BlockSpec((tm, d), lambda i, j: (i, 0)), y_spec, y_spec, y_spec, y_spec,
                  gate_spec(0), gate_spec(1), gate_spec(2), gate_spec(3),
                  pl.BlockSpec((None, N_BRANCH, BRANCH_W, tn), lambda i, j: (layer, 0, 0, j)),
                  pl.BlockSpec((None, N_BRANCH, tn), lambda i, j: (layer, 0, j))],
        out_specs=pl.BlockSpec((tm, tn), lambda i, j: (i, j)),
        compiler_params=_params(("arbitrary", "arbitrary"), blocks, 6 * _nbytes((tm, tn), F32)),
        name="merge",
    )(h, *ys, w_gate, w_gate, w_gate, w_gate, w_branch, b_gate)


def _ffn_in_kernel(h_ref, wg_ref, wu_ref, o_ref):
    h = h_ref[...]
    g = jnp.dot(h, wg_ref[...], preferred_element_type=F32)
    u = jnp.dot(h, wu_ref[...], preferred_element_type=F32)
    o_ref[...] = (jax.nn.silu(g) * u).astype(o_ref.dtype)


def _ffn_in(h, w_ffn_in, layer, *, tm, tn):
    t, d = h.shape
    per = FFN_HIDDEN // tn
    blocks = [_nbytes((tm, d), BF16), _nbytes((d, tn), BF16), _nbytes((d, tn), BF16), _nbytes((tm, tn), BF16)]
    return pl.pallas_call(
        _ffn_in_kernel,
        out_shape=jax.ShapeDtypeStruct((t, FFN_HIDDEN), BF16),
        grid=(t // tm, per),
        in_specs=[pl.BlockSpec((tm, d), lambda i, j: (i, 0)),
                  pl.BlockSpec((None, d, tn), lambda i, j: (layer, 0, j)),
                  pl.BlockSpec((None, d, tn), lambda i, j: (layer, 0, per + j))],
        out_specs=pl.BlockSpec((tm, tn), lambda i, j: (i, j)),
        compiler_params=_params(("arbitrary", "arbitrary"), blocks, 4 * _nbytes((tm, tn), F32)),
        name="ffn_in",
    )(h, w_ffn_in, w_ffn_in)


def _rope_tables(n_lat_rows, n_ctx_rows):
    half = A_HEAD_DIM // 4
    t = jnp.arange(n_lat_rows)
    freqs = ROPE_THETA ** (-jnp.arange(half, dtype=F32) / half)

    def cos_sin(pos):
        ang = pos.astype(F32)[:, None] * freqs[None, :]
        return jnp.cos(ang), jnp.sin(ang)

    cr, sr = cos_sin(t // GRID_W)
    cc, sc = cos_sin(t % GRID_W)
    cos = jnp.concatenate([cr, cr, cc, cc], axis=-1)
    sin = jnp.concatenate([-sr, sr, -sc, sc], axis=-1)
    cos = jnp.concatenate([cos, jnp.ones((n_ctx_rows, A_HEAD_DIM), F32)], axis=0)
    sin = jnp.concatenate([sin, jnp.zeros((n_ctx_rows, A_HEAD_DIM), F32)], axis=0)
    reps = LANES // A_HEAD_DIM
    return jnp.tile(cos, (1, reps)), jnp.tile(sin, (1, reps))


def kernel(x, c, ctx, c_ctx, w_mod, b_mod, norm1, norm2, w_in, b_gate, a_sink, nb_bias, c_conv_w, c_conv_b,
           c_w_a, c_b_a, c_w_x, c_b_x, c_lam, d_conv_w, w_branch, w_out, w_ffn_in, w_ffn_out, final_norm):
    bsz, s, d = x.shape
    l_ctx = ctx.shape[1]
    assert bsz == 1 and d == D_MODEL and c.shape[0] == 1
    assert s % ROW_CHUNK == 0 and l_ctx % ROW_CHUNK == 0 and s // GRID_W >= NB_ROWS and s >= 3 * A_BLOCK
    t = s + l_ctx
    depth = w_mod.shape[0]

    w_a_cols = w_in[:, :, :A_COLS].astype(BF16)
    w_b_cols = w_in[:, :, A_COLS:A_COLS + B_COLS].astype(BF16)
    w_cd_cols = w_in[:, :, A_COLS + B_COLS:GATE_COL0].astype(BF16)
    w_gate = w_in[:, :, GATE_COL0:].astype(BF16)
    w_branch_b = w_branch.astype(BF16)
    w_out_b = w_out.astype(BF16)
    w_ffn_in_b = w_ffn_in.astype(BF16)
    w_ffn_out_b = w_ffn_out.astype(BF16)
    wg_scan = jnp.concatenate([c_w_a, c_w_x], axis=-1).astype(BF16)

    cos_t, sin_t = _rope_tables(s, l_ctx)
    bias_tabs = _nb_bias_tables(nb_bias)

    cs = jnp.concatenate([c, c_ctx[None, :], jnp.zeros((SUBLANES - 2, d), F32)], axis=0)
    mod_all = _modulation(cs, w_mod, b_mod)
    mod_all = mod_all[:, :2].reshape(depth, 2, N_MOD, d).transpose(0, 2, 1, 3)

    xt = jnp.concatenate([x[0], ctx[0]], axis=0)

    tm_big = _pick_tile(t, 1408)
    tm_small = _pick_tile(t, 704)
    for l in range(depth):
        mod = mod_all[l]
        h = _norm_mod(xt, norm1[l], mod, 0, s)
        za = _matmul(h, w_a_cols, l, tm=tm_big, tn=A_COLS, out_dtype=BF16, name="proj_a", rope=(cos_t, sin_t))
        zb = _matmul(h, w_b_cols, l, tm=tm_big, tn=B_COLS // 2, out_dtype=BF16, name="proj_b")
        zcd = _matmul(h, w_cd_cols, l, tm=tm_big, tn=BRANCH_W, out_dtype=F32, name="proj_cd")
        ya = _attn_a(za, a_sink[l], s)
        yb = _attn_b(zb, bias_tabs, l, s)
        u, yd = _cd_prep(zcd, c_conv_w[l], c_conv_b[l], d_conv_w[l], s)
        hf, hb = _rglru_scan(u, wg_scan[l], c_b_a[l], c_b_x[l], c_lam[l], s)
        yr = _c_final(hf, hb, zcd)
        merged = _merge(h, (ya, yb, yr, yd), w_gate, w_branch_b, b_gate, l, tm=tm_small, tn=256)
        xt = _matmul(merged, w_out_b, l, tm=tm_big, tn=512, out_dtype=F32, name="proj_out",
                     resid=(xt, mod, 2, s))
        h2 = _norm_mod(xt, norm2[l], mod, 3, s)
        act = _ffn_in(h2, w_ffn_in_b, l, tm=tm_big, tn=512)
        xt = _matmul(act, w_ffn_out_b, l, tm=tm_small, tn=512, out_dtype=F32, name="ffn_out",
                     resid=(xt, mod, 5, s))
    return _final_norm(xt, final_norm, s)[None]
```

```python
import functools

import numpy as np
import jax
import jax.numpy as jnp
from jax import lax
from jax.experimental import pallas as pl
from jax.experimental.pallas import tpu as pltpu

F32 = jnp.float32
BF16 = jnp.bfloat16

D_MODEL = 2048
DEPTH = 4
GRID_W = 64
EPS = 1e-6
N_MOD = 6
N_BRANCH = 4
BRANCH_W = 512
A_HEADS, A_KV_HEADS, A_HEAD_DIM, A_WINDOW, A_BLOCK = 8, 2, 64, 128, 128
ROPE_THETA = 10000.0
B_HEADS, B_HEAD_DIM, NB_ROWS, NB_COLS = 8, 64, 8, 16
C_BLOCKS, C_BLOCK_W, C_CONV, C_POW = 4, 128, 4, 8.0
D_CONV = 3
FFN_HIDDEN = 5632
A_COLS = (A_HEADS + 2 * A_KV_HEADS) * A_HEAD_DIM
A_ROPE_COLS = (A_HEADS + A_KV_HEADS) * A_HEAD_DIM
B_COLS = 3 * BRANCH_W
CD_COLS = 5 * BRANCH_W
GATE_COL0 = A_COLS + B_COLS + CD_COLS

LANES = 128
SUBLANES = 8
BF16_ROWS = 16
VMEM_BYTES = 64 * 1024 * 1024
VMEM_CAP = VMEM_BYTES - 8 * 1024 * 1024

ROW_CHUNK = 256
ATTN_ROWS = 256


def _pick_tile(total, target, mult=BF16_ROWS):
    best = None
    for d in range(mult, min(total, target) + 1, mult):
        if total % d == 0:
            best = d
    assert best is not None, (total, target, mult)
    return best


def _params(semantics, block_bytes, extra_bytes=0):
    need = 2 * sum(block_bytes) + extra_bytes + (4 << 20)
    return pltpu.CompilerParams(dimension_semantics=semantics,
                                vmem_limit_bytes=int(min(max(need, 16 << 20), VMEM_CAP)))


def _nbytes(shape, dtype):
    return int(np.prod(shape)) * jnp.dtype(dtype).itemsize


def _mod_kernel(cs_ref, w_ref, b_ref, o_ref):
    s = jax.nn.silu(cs_ref[...]).astype(BF16)
    w = w_ref[0].astype(BF16)
    o_ref[0] = jnp.dot(s, w, preferred_element_type=F32) + b_ref[0]


def _modulation(cs, w_mod, b_mod):
    depth, d, n = w_mod.shape
    tn = 1024
    return pl.pallas_call(
        _mod_kernel,
        out_shape=jax.ShapeDtypeStruct((depth, SUBLANES, n), F32),
        grid=(depth, n // tn),
        in_specs=[pl.BlockSpec((SUBLANES, d), lambda l, j: (0, 0)),
                  pl.BlockSpec((1, d, tn), lambda l, j: (l, 0, j)),
                  pl.BlockSpec((1, 1, tn), lambda l, j: (l, 0, j))],
        out_specs=pl.BlockSpec((1, SUBLANES, tn), lambda l, j: (l, 0, j)),
        compiler_params=_params(("arbitrary", "arbitrary"), [_nbytes((d, tn), F32)],
                                _nbytes((d, tn), BF16)),
        name="modulation",
    )(cs, w_mod, b_mod.reshape(depth, 1, n))


def _norm_mod_kernel(x_ref, g_ref, mod_ref, o_ref, *, shift_idx, n_lat_blocks):
    x = x_ref[...]
    y = x * lax.rsqrt(jnp.mean(x * x, axis=-1, keepdims=True) + EPS)
    y = y * g_ref[...]
    is_ctx = pl.program_id(0) >= n_lat_blocks
    shift2 = mod_ref[shift_idx]
    scale2 = mod_ref[shift_idx + 1]
    shift = jnp.where(is_ctx, shift2[1:2], shift2[0:1])
    scale = jnp.where(is_ctx, scale2[1:2], scale2[0:1])
    o_ref[...] = (y * (1 + scale) + shift).astype(o_ref.dtype)


def _norm_mod(x, g, mod, shift_idx, n_lat_rows):
    t, d = x.shape
    return pl.pallas_call(
        functools.partial(_norm_mod_kernel, shift_idx=shift_idx,
                          n_lat_blocks=n_lat_rows // ROW_CHUNK),
        out_shape=jax.ShapeDtypeStruct((t, d), BF16),
        grid=(t // ROW_CHUNK,),
        in_specs=[pl.BlockSpec((ROW_CHUNK, d), lambda i: (i, 0)),
                  pl.BlockSpec((1, d), lambda i: (0, 0)),
                  pl.BlockSpec((N_MOD, 2, d), lambda i: (0, 0, 0))],
        out_specs=pl.BlockSpec((ROW_CHUNK, d), lambda i: (i, 0)),
        compiler_params=_params(("arbitrary",), [_nbytes((ROW_CHUNK, d), F32)] * 2,
                                4 * _nbytes((ROW_CHUNK, d), F32)),
        name="norm_mod",
    )(x, g.reshape(1, d), mod)


def _rope_rotate_half(x):
    lane = lax.broadcasted_iota(jnp.int32, x.shape, 1)
    first = (lane & 16) == 0
    return jnp.where(first, pltpu.roll(x, LANES - 16, 1), pltpu.roll(x, 16, 1))


def _mm_cast_kernel(a_ref, w_ref, o_ref):
    o_ref[...] = jnp.dot(a_ref[...], w_ref[...], preferred_element_type=F32).astype(o_ref.dtype)


def _mm_rope_kernel(a_ref, w_ref, cos_ref, sin_ref, o_ref):
    acc = jnp.dot(a_ref[...], w_ref[...], preferred_element_type=F32)
    cos = cos_ref[...]
    sin = sin_ref[...]
    for c in range(A_ROPE_COLS // LANES):
        xc = acc[:, c * LANES:(c + 1) * LANES]
        o_ref[:, c * LANES:(c + 1) * LANES] = (xc * cos + _rope_rotate_half(xc) * sin).astype(o_ref.dtype)
    o_ref[:, A_ROPE_COLS:] = acc[:, A_ROPE_COLS:].astype(o_ref.dtype)


def _matmul(a, w, layer, *, tm, tn, out_dtype, name, rope=None):
    t, k = a.shape
    n = w.shape[-1]
    grid = (t // tm, n // tn)
    in_specs = [pl.BlockSpec((tm, k), lambda i, j: (i, 0)),
                pl.BlockSpec((None, k, tn), lambda i, j: (layer, 0, j))]
    operands = [a, w]
    blocks = [_nbytes((tm, k), BF16), _nbytes((k, tn), BF16), _nbytes((tm, tn), out_dtype)]
    if rope is not None:
        kernel = _mm_rope_kernel
        in_specs += [pl.BlockSpec((tm, LANES), lambda i, j: (i, 0))] * 2
        operands += list(rope)
    else:
        kernel = _mm_cast_kernel
    return pl.pallas_call(
        kernel,
        out_shape=jax.ShapeDtypeStruct((t, n), out_dtype),
        grid=grid,
        in_specs=in_specs,
        out_specs=pl.BlockSpec((tm, tn), lambda i, j: (i, j)),
        compiler_params=_params(("arbitrary", "arbitrary"), blocks, 3 * _nbytes((tm, tn), F32)),
        name=name,
    )(*operands)


CAST_ROWS = 128


def _pick_row(two_rows, is_ctx):
    return jnp.where(is_ctx, two_rows[1:2], two_rows[0:1])


def _rms(x, g):
    y = x * lax.rsqrt(jnp.mean(x * x, axis=-1, keepdims=True) + EPS)
    return y * g


def _resident_bf16(w_ref, scratch):
    if not scratch:
        return w_ref
    wb_ref, = scratch

    @pl.when(pl.program_id(0) == 0)
    def _():
        def body(c, carry):
            r = pl.multiple_of(c * CAST_ROWS, CAST_ROWS)
            wb_ref[pl.ds(r, CAST_ROWS), :] = w_ref[pl.ds(r, CAST_ROWS), :].astype(BF16)
            return carry
        lax.fori_loop(0, w_ref.shape[0] // CAST_ROWS, body, 0)

    return wb_ref


def _resid_norm_kernel(a_ref, w_ref, x_ref, gmod_ref, nmod_ref, g_ref, xo_ref, ho_ref, *scratch,
                       gate_idx, shift_idx, n_lat_blocks):
    w = _resident_bf16(w_ref, scratch)
    is_ctx = pl.program_id(0) >= n_lat_blocks
    acc = jnp.dot(a_ref[...], w[...], preferred_element_type=F32)
    x = x_ref[...] + _pick_row(gmod_ref[gate_idx], is_ctx) * acc
    xo_ref[...] = x
    y = _rms(x, g_ref[...])
    ho_ref[...] = (y * (1 + _pick_row(nmod_ref[shift_idx + 1], is_ctx))
                   + _pick_row(nmod_ref[shift_idx], is_ctx)).astype(ho_ref.dtype)


def _resid_final_kernel(a_ref, w_ref, x_ref, gmod_ref, g_ref, o_ref, *, gate_idx):
    acc = jnp.dot(a_ref[...], w_ref[...], preferred_element_type=F32)
    x = x_ref[...] + gmod_ref[gate_idx][0:1] * acc
    o_ref[...] = _rms(x, g_ref[...])


def _resid_norm(a, w, layer, x, gmod, gate_idx, g, nmod, shift_idx, n_lat_rows, *, name):
    t, k = a.shape
    d = x.shape[1]
    tm = ROW_CHUNK
    cast_w = w.dtype != BF16
    rows = lambda width: pl.BlockSpec((tm, width), lambda i: (i, 0))
    mod_spec = pl.BlockSpec((N_MOD, 2, d), lambda i: (0, 0, 0))
    resident = _nbytes((k, d), w.dtype) + (_nbytes((k, d), BF16) if cast_w else 0)
    return pl.pallas_call(
        functools.partial(_resid_norm_kernel, gate_idx=gate_idx, shift_idx=shift_idx,
                          n_lat_blocks=n_lat_rows // tm),
        out_shape=(jax.ShapeDtypeStruct((t, d), F32), jax.ShapeDtypeStruct((t, d), BF16)),
        grid=(t // tm,),
        in_specs=[rows(k),
                  pl.BlockSpec((None, k, d), lambda i: (layer, 0, 0), pipeline_mode=pl.Buffered(1)),
                  rows(d), mod_spec, mod_spec, pl.BlockSpec((1, d), lambda i: (0, 0))],
        out_specs=(rows(d), rows(d)),
        scratch_shapes=[pltpu.VMEM((k, d), BF16)] if cast_w else [],
        compiler_params=_params(("arbitrary",),
                                [_nbytes((tm, k), BF16), _nbytes((tm, d), F32) * 2, _nbytes((tm, d), BF16)],
                                resident + 4 * _nbytes((tm, d), F32)),
        name=name,
    )(a, w, x, gmod, nmod, g.reshape(1, d))


def _resid_final(a, w, layer, x, gmod, gate_idx, g, n_rows, *, name):
    k = a.shape[1]
    d = x.shape[1]
    tm = ROW_CHUNK
    rows = lambda width: pl.BlockSpec((tm, width), lambda i: (i, 0))
    return pl.pallas_call(
        functools.partial(_resid_final_kernel, gate_idx=gate_idx),
        out_shape=jax.ShapeDtypeStruct((n_rows, d), F32),
        grid=(n_rows // tm,),
        in_specs=[rows(k),
                  pl.BlockSpec((None, k, d), lambda i: (layer, 0, 0), pipeline_mode=pl.Buffered(1)),
                  rows(d), pl.BlockSpec((N_MOD, 2, d), lambda i: (0, 0, 0)),
                  pl.BlockSpec((1, d), lambda i: (0, 0))],
        out_specs=rows(d),
        compiler_params=_params(("arbitrary",), [_nbytes((tm, k), BF16), _nbytes((tm, d), F32) * 2],
                                _nbytes((k, d), BF16) + 4 * _nbytes((tm, d), F32)),
        name=name,
    )(a, w, x, gmod, g.reshape(1, d))


def _dot_nt(a, b):
    return lax.dot_general(a, b, (((1,), (1,)), ((), ())), preferred_element_type=F32)


def _attend(q, kb, vb, kc, vc, bias, sink=None):
    s_loc = _dot_nt(q, kb) + bias
    s_ctx = _dot_nt(q, kc)
    m = jnp.maximum(jnp.max(s_loc, axis=-1, keepdims=True), jnp.max(s_ctx, axis=-1, keepdims=True))
    if sink is not None:
        m = jnp.maximum(m, sink)
    e_loc = jnp.exp(s_loc - m)
    e_ctx = jnp.exp(s_ctx - m)
    den = jnp.sum(e_loc, axis=-1, keepdims=True) + jnp.sum(e_ctx, axis=-1, keepdims=True)
    if sink is not None:
        den = den + jnp.exp(sink - m)
    o = (jnp.dot(e_loc.astype(BF16), vb, preferred_element_type=F32)
         + jnp.dot(e_ctx.astype(BF16), vc, preferred_element_type=F32))
    return o / den


def _attn_a_kernel(sink_ref, q_ref, k_ref, v_ref, o_ref, *, n_lat_rows, n_ctx_rows):
    n = pl.program_id(0)
    rows = q_ref.shape[0]
    band = rows + 2 * A_WINDOW
    is_lat = n * rows < n_lat_rows
    start = pl.multiple_of(jnp.clip(n * rows - A_WINDOW, 0, n_lat_rows - band), A_BLOCK)
    kb = k_ref[pl.ds(start, band), :]
    vb = v_ref[pl.ds(start, band), :]
    kc = k_ref[pl.ds(n_lat_rows, n_ctx_rows), :]
    vc = v_ref[pl.ds(n_lat_rows, n_ctx_rows), :]
    qpos = n * rows + lax.broadcasted_iota(jnp.int32, (rows, band), 0)
    kpos = start + lax.broadcasted_iota(jnp.int32, (rows, band), 1)
    bias = jnp.where((jnp.abs(qpos - kpos) <= A_WINDOW) & is_lat, 0.0, -jnp.inf)
    scale = A_HEAD_DIM ** -0.5
    grp = A_HEADS // A_KV_HEADS
    q = q_ref[...] * scale
    outs = []
    for h in range(A_HEADS):
        kv = slice((h // grp) * A_HEAD_DIM, (h // grp + 1) * A_HEAD_DIM)
        outs.append(_attend(q[:, h * A_HEAD_DIM:(h + 1) * A_HEAD_DIM],
                            kb[:, kv], vb[:, kv], kc[:, kv], vc[:, kv], bias, sink_ref[h]))
    o_ref[...] = jnp.concatenate(outs, axis=1).astype(o_ref.dtype)


def _attn_a(za, sink, n_lat_rows):
    t = za.shape[0]
    qw = A_HEADS * A_HEAD_DIM
    kvw = A_KV_HEADS * A_HEAD_DIM
    assert A_WINDOW == A_BLOCK and n_lat_rows >= ATTN_ROWS + 2 * A_WINDOW
    resident = pl.Buffered(1)
    return pl.pallas_call(
        functools.partial(_attn_a_kernel, n_lat_rows=n_lat_rows, n_ctx_rows=t - n_lat_rows),
        out_shape=jax.ShapeDtypeStruct((t, qw), BF16),
        grid=(t // ATTN_ROWS,),
        in_specs=[pl.BlockSpec(memory_space=pltpu.SMEM),
                  pl.BlockSpec((ATTN_ROWS, qw), lambda n: (n, 0)),
                  pl.BlockSpec((t, kvw), lambda n: (0, qw // kvw), pipeline_mode=resident),
                  pl.BlockSpec((t, kvw), lambda n: (0, qw // kvw + 1), pipeline_mode=resident)],
        out_specs=pl.BlockSpec((ATTN_ROWS, qw), lambda n: (n, 0)),
        compiler_params=_params(("arbitrary",), [_nbytes((t, kvw), BF16)] * 2, 16 << 20),
        name="attn_a",
    )(sink, za, za, za)


B_GROUP = ATTN_ROWS // GRID_W
B_SPAN = 12
B_TABLE_FIRST, B_TABLE_INNER, B_TABLE_LAST, B_TABLE_CTX = range(4)


def _nb_bias_tables(rel_bias):
    assert B_GROUP == NB_ROWS // 2 and B_SPAN >= NB_ROWS + B_GROUP - 1 and (B_SPAN * GRID_W) % LANES == 0
    cq = np.arange(GRID_W)
    cstart = np.clip(cq - NB_COLS // 2, 0, GRID_W - NB_COLS)
    col_ok = (cq[None, :] >= cstart[:, None]) & (cq[None, :] < cstart[:, None] + NB_COLS)
    b = rel_bias.astype(F32)
    pad = GRID_W - NB_COLS
    bpad = jnp.concatenate([jnp.repeat(b[..., :1], pad, axis=-1), b,
                            jnp.repeat(b[..., -1:], pad, axis=-1)], axis=-1)
    off = pad + NB_COLS - 1
    toep = jnp.stack([bpad[..., off - q:off - q + GRID_W] for q in range(GRID_W)], axis=-2)
    toep = jnp.where(col_ok, toep, -jnp.inf)
    hidden = jnp.full_like(toep[..., 0, :, :], -jnp.inf)
    layouts = {B_TABLE_FIRST: (lambda j: 0, NB_ROWS - 1),
               B_TABLE_INNER: (lambda j: j, NB_ROWS - 1 - NB_ROWS // 2),
               B_TABLE_LAST: (lambda j: B_SPAN - NB_ROWS, NB_ROWS - 1 + B_GROUP - B_SPAN)}
    tabs = []
    for kind in (B_TABLE_FIRST, B_TABLE_INNER, B_TABLE_LAST):
        first_row, dr0 = layouts[kind]
        rows = []
        for j in range(B_GROUP):
            blocks = [toep[..., lj - j + dr0, :, :] if first_row(j) <= lj < first_row(j) + NB_ROWS else hidden
                      for lj in range(B_SPAN)]
            rows.append(jnp.concatenate(blocks, axis=-1))
        tabs.append(jnp.concatenate(rows, axis=-2))
    tabs.append(jnp.full_like(tabs[0], -jnp.inf))
    return jnp.stack(tabs, axis=-4)


def _attn_b_kernel(q_ref, k_ref, v_ref, bias_ref, o_ref, *, n_lat_rows, n_ctx_rows):
    g = pl.program_id(0)
    grid_rows = n_lat_rows // GRID_W
    win = B_SPAN * GRID_W
    first = jnp.clip(g * B_GROUP - NB_ROWS // 2, 0, grid_rows - B_SPAN)
    start = pl.multiple_of(first * GRID_W, GRID_W)
    kb = k_ref[pl.ds(start, win), :]
    vb = v_ref[pl.ds(start, win), :]
    kc = k_ref[pl.ds(n_lat_rows, n_ctx_rows), :]
    vc = v_ref[pl.ds(n_lat_rows, n_ctx_rows), :]
    q = q_ref[...] * (B_HEAD_DIM ** -0.5)
    outs = []
    for h in range(B_HEADS):
        hs = slice(h * B_HEAD_DIM, (h + 1) * B_HEAD_DIM)
        outs.append(_attend(q[:, hs], kb[:, hs], vb[:, hs], kc[:, hs], vc[:, hs], bias_ref[0, h]))
    o_ref[...] = jnp.concatenate(outs, axis=1).astype(o_ref.dtype)


def _attn_b(zb, bias_tabs, layer, n_lat_rows):
    t = zb.shape[0]
    w = BRANCH_W
    n_lat_groups = n_lat_rows // ATTN_ROWS
    win = B_SPAN * GRID_W
    assert n_lat_rows // GRID_W >= B_SPAN and n_lat_groups >= 2

    def table_index(g):
        inner = jnp.where(g == 0, B_TABLE_FIRST, jnp.where(g == n_lat_groups - 1, B_TABLE_LAST, B_TABLE_INNER))
        return jnp.where(g >= n_lat_groups, B_TABLE_CTX, inner)

    resident = pl.Buffered(1)
    table_block = (None, 1, B_HEADS, ATTN_ROWS, win)
    return pl.pallas_call(
        functools.partial(_attn_b_kernel, n_lat_rows=n_lat_rows, n_ctx_rows=t - n_lat_rows),
        out_shape=jax.ShapeDtypeStruct((t, w), BF16),
        grid=(t // ATTN_ROWS,),
        in_specs=[pl.BlockSpec((ATTN_ROWS, w), lambda g: (g, 0)),
                  pl.BlockSpec((t, w), lambda g: (0, 1), pipeline_mode=resident),
                  pl.BlockSpec((t, w), lambda g: (0, 2), pipeline_mode=resident),
                  pl.BlockSpec(table_block, lambda g: (layer, table_index(g), 0, 0, 0))],
        out_specs=pl.BlockSpec((ATTN_ROWS, w), lambda g: (g, 0)),
        compiler_params=_params(("arbitrary",),
                                [_nbytes((t, w), BF16), _nbytes((B_HEADS, ATTN_ROWS, win), F32)],
                                16 << 20),
        name="attn_b",
    )(zb, zb, zb, bias_tabs)


def _cd_prep_kernel(xr_ref, xr_p, xr_n, xd_ref, xd_p, xd_n, cd_ref, cd_p, cd_n, bd_ref,
                    cw_ref, cb_ref, dw_ref, u_ref, yd_ref, *, n_lat_chunks, n_chunks):
    c = pl.program_id(0)
    has_prev = (c != 0) & (c != n_lat_chunks)
    has_next = (c != n_lat_chunks - 1) & (c != n_chunks - 1)
    ch = xr_ref.shape[0]
    h = SUBLANES

    def halo_cat(cur, prev, nxt):
        return jnp.concatenate([jnp.where(has_prev, prev, 0.0), cur, jnp.where(has_next, nxt, 0.0)], axis=0)

    xr = halo_cat(xr_ref[...], xr_p[...], xr_n[...])
    left = C_CONV // 2
    u = xr[h - left:h - left + ch] * cw_ref[0:1, :]
    for j in range(1, C_CONV):
        u = u + xr[h - left + j:h - left + j + ch] * cw_ref[j:j + 1, :]
    u_ref[...] = u + cb_ref[...]

    pd = halo_cat(cd_ref[...] * xd_ref[...], cd_p[...] * xd_p[...], cd_n[...] * xd_n[...])
    left = D_CONV // 2
    y = pd[h - left:h - left + ch] * dw_ref[0:1, :]
    for j in range(1, D_CONV):
        y = y + pd[h - left + j:h - left + j + ch] * dw_ref[j:j + 1, :]
    yd_ref[...] = (bd_ref[...] * y).astype(yd_ref.dtype)


def _cd_prep(zcd, conv_w, conv_b, d_conv_w, n_lat_rows):
    t = zcd.shape[0]
    w = BRANCH_W
    ch = ROW_CHUNK
    n_chunks = t // ch
    per = ch // SUBLANES
    last = t // SUBLANES - 1

    def cur(col):
        return pl.BlockSpec((ch, w), lambda c: (c, col))

    def prev(col):
        return pl.BlockSpec((SUBLANES, w), lambda c: (jnp.maximum(c * per - 1, 0), col))

    def nxt(col):
        return pl.BlockSpec((SUBLANES, w), lambda c: (jnp.minimum((c + 1) * per, last), col))

    small = lambda rows: pl.BlockSpec((rows, w), lambda c: (0, 0))
    return pl.pallas_call(
        functools.partial(_cd_prep_kernel, n_lat_chunks=n_lat_rows // ch, n_chunks=n_chunks),
        out_shape=(jax.ShapeDtypeStruct((t, w), F32), jax.ShapeDtypeStruct((t, w), BF16)),
        grid=(n_chunks,),
        in_specs=[cur(0), prev(0), nxt(0), cur(2), prev(2), nxt(2), cur(4), prev(4), nxt(4), cur(3),
                  small(C_CONV), small(1), small(D_CONV)],
        out_specs=(pl.BlockSpec((ch, w), lambda c: (c, 0)), pl.BlockSpec((ch, w), lambda c: (c, 0))),
        compiler_params=_params(("arbitrary",), [_nbytes((ch, w), F32)] * 6, 8 << 20),
        name="cd_prep",
    )(zcd, zcd, zcd, zcd, zcd, zcd, zcd, zcd, zcd, zcd, conv_w, conv_b.reshape(1, w), d_conv_w)


def _tile_scan(a, b, reverse):
    rows = lax.broadcasted_iota(jnp.int32, a.shape, 0)
    for k in (1, 2, 4):
        if reverse:
            a_sh, b_sh, valid = pltpu.roll(a, SUBLANES - k, 0), pltpu.roll(b, SUBLANES - k, 0), rows < SUBLANES - k
        else:
            a_sh, b_sh, valid = pltpu.roll(a, k, 0), pltpu.roll(b, k, 0), rows >= k
        b = jnp.where(valid, a * b_sh + b, b)
        a = jnp.where(valid, a * a_sh, a)
    return a, b


def _scan_kernel(uf_ref, ub_ref, wg_ref, ba_ref, bx_ref, lam_ref, hf_ref, hb_ref,
                 a_scr, b_scr, carry_f, carry_b):
    @pl.when(pl.program_id(0) == 0)
    def _():
        carry_f[...] = jnp.zeros_like(carry_f)
        carry_b[...] = jnp.zeros_like(carry_b)

    ch, w = uf_ref.shape
    n_tiles = ch // SUBLANES

    def gates(u_ref, d):
        u = u_ref[...]
        ub = u.astype(BF16)
        sp = jax.nn.softplus(-lam_ref[d:d + 1, :])
        for blk in range(C_BLOCKS):
            sl = slice(blk * C_BLOCK_W, (blk + 1) * C_BLOCK_W)
            g = jnp.dot(ub[:, sl], wg_ref[d, blk], preferred_element_type=F32)
            r = jax.nn.sigmoid(g[:, :C_BLOCK_W] + ba_ref[d:d + 1, sl])
            i = jax.nn.sigmoid(g[:, C_BLOCK_W:] + bx_ref[d:d + 1, sl])
            log_a = -C_POW * r * sp[:, sl]
            a = jnp.exp(log_a)
            a_scr[:, sl] = a
            b_scr[:, sl] = jnp.sqrt(-jnp.tanh(log_a) * (a * a + 1.0)) * (i * u[:, sl])

    def scan(out_ref, carry_ref, reverse):
        def body(step, carry):
            tile = n_tiles - 1 - step if reverse else step
            r0 = pl.multiple_of(tile * SUBLANES, SUBLANES)
            a, b = _tile_scan(a_scr[pl.ds(r0, SUBLANES), :], b_scr[pl.ds(r0, SUBLANES), :], reverse)
            h = a * carry + b
            out_ref[pl.ds(r0, SUBLANES), :] = h
            edge = h[0:1, :] if reverse else h[SUBLANES - 1:SUBLANES, :]
            return jnp.broadcast_to(edge, h.shape)

        carry_ref[...] = lax.fori_loop(0, n_tiles, body, carry_ref[...], unroll=4)

    gates(uf_ref, 0)
    scan(hf_ref, carry_f, False)
    gates(ub_ref, 1)
    scan(hb_ref, carry_b, True)


def _rglru_scan(u, wg, b_a, b_x, lam, n_lat_rows):
    t, w = u.shape
    ch = ROW_CHUNK
    n_chunks = t // ch
    n_lat = n_lat_rows // ch
    n_ctx = n_chunks - n_lat

    def fwd_chunk(i):
        return jnp.where(i < n_ctx, n_lat + i, i - n_ctx)

    def bwd_chunk(i):
        return n_chunks - 1 - i

    vec = pl.BlockSpec((2, w), lambda i: (0, 0))
    return pl.pallas_call(
        _scan_kernel,
        out_shape=(jax.ShapeDtypeStruct((t, w), F32), jax.ShapeDtypeStruct((t, w), F32)),
        grid=(n_chunks,),
        in_specs=[pl.BlockSpec((ch, w), lambda i: (fwd_chunk(i), 0)),
                  pl.BlockSpec((ch, w), lambda i: (bwd_chunk(i), 0)),
                  pl.BlockSpec((2, C_BLOCKS, C_BLOCK_W, 2 * C_BLOCK_W), lambda i: (0, 0, 0, 0)),
                  vec, vec, vec],
        out_specs=(pl.BlockSpec((ch, w), lambda i: (fwd_chunk(i), 0)),
                   pl.BlockSpec((ch, w), lambda i: (bwd_chunk(i), 0))),
        scratch_shapes=[pltpu.VMEM((ch, w), F32), pltpu.VMEM((ch, w), F32),
                        pltpu.VMEM((SUBLANES, w), F32), pltpu.VMEM((SUBLANES, w), F32)],
        compiler_params=_params(("arbitrary",), [_nbytes((ch, w), F32)] * 4, 8 << 20),
        name="rglru_scan",
    )(u, u, wg, b_a, b_x, lam)


def _c_final_kernel(hf_ref, hb_ref, gr_ref, o_ref):
    o_ref[...] = ((hf_ref[...] + hb_ref[...]) * jax.nn.gelu(gr_ref[...])).astype(o_ref.dtype)


def _c_final(hf, hb, zcd):
    t, w = hf.shape
    rows = _pick_tile(t, 1024, SUBLANES)
    spec = pl.BlockSpec((rows, w), lambda i: (i, 0))
    return pl.pallas_call(
        _c_final_kernel,
        out_shape=jax.ShapeDtypeStruct((t, w), BF16),
        grid=(t // rows,),
        in_specs=[spec, spec, pl.BlockSpec((rows, w), lambda i: (i, 1))],
        out_specs=spec,
        compiler_params=_params(("arbitrary",), [_nbytes((rows, w), F32)] * 4, 4 << 20),
        name="c_final",
    )(hf, hb, zcd)


def _merge_kernel(h_ref, ya_ref, yb_ref, yr_ref, yd_ref, wg0, wg1, wg2, wg3, wb_ref, bg_ref, o_ref):
    h = h_ref[...]
    merged = None
    for k, (y_ref, wg_ref) in enumerate(zip((ya_ref, yb_ref, yr_ref, yd_ref), (wg0, wg1, wg2, wg3))):
        g = jax.nn.sigmoid(jnp.dot(h, wg_ref[...], preferred_element_type=F32) + bg_ref[k:k + 1, :])
        term = g * jnp.dot(y_ref[...], wb_ref[k], preferred_element_type=F32)
        merged = term if merged is None else merged + term
    o_ref[...] = merged.astype(o_ref.dtype)


def _merge(h, ys, w_gate, w_branch, b_gate, layer, *, tm, tn):
    t, d = h.shape
    per = d // tn
    y_spec = pl.BlockSpec((tm, BRANCH_W), lambda i, j: (i, 0))

    def gate_spec(k):
        return pl.BlockSpec((None, d, tn), lambda i, j: (layer, 0, k * per + j))

    blocks = ([_nbytes((tm, d), BF16)] + [_nbytes((tm, BRANCH_W), BF16)] * N_BRANCH
              + [_nbytes((d, tn), BF16)] * N_BRANCH + [_nbytes((N_BRANCH, BRANCH_W, tn), BF16)]
              + [_nbytes((tm, tn), BF16)])
    return pl.pallas_call(
        _merge_kernel,
        out_shape=jax.ShapeDtypeStruct((t, d), BF16),
        grid=(t // tm, per),
        in_specs=[pl.BlockSpec((tm, d), lambda i, j: (i, 0)), y_spec, y_spec, y_spec, y_spec,
                  gate_spec(0), gate_spec(1), gate_spec(2), gate_spec(3),
                  pl.BlockSpec((None, N_BRANCH, BRANCH_W, tn), lambda i, j: (layer, 0, 0, j)),
                  pl.BlockSpec((None, N_BRANCH, tn), lambda i, j: (layer, 0, j))],
        out_specs=pl.BlockSpec((tm, tn), lambda i, j: (i, j)),
        compiler_params=_params(("arbitrary", "arbitrary"), blocks, 6 * _nbytes((tm, tn), F32)),
        name="merge",
    )(h, *ys, w_gate, w_gate, w_gate, w_gate, w_branch, b_gate)


def _ffn_in_kernel(h_ref, wg_ref, wu_ref, o_ref):
    h = h_ref[...]
    g = jnp.dot(h, wg_ref[...].astype(BF16), preferred_element_type=F32)
    u = jnp.dot(h, wu_ref[...].astype(BF16), preferred_element_type=F32)
    o_ref[...] = (jax.nn.silu(g) * u).astype(o_ref.dtype)


def _ffn_in(h, w_ffn_in, layer, *, tm, tn):
    t, d = h.shape
    per = FFN_HIDDEN // tn
    blocks = [_nbytes((tm, d), BF16), _nbytes((d, tn), F32), _nbytes((d, tn), F32), _nbytes((tm, tn), BF16)]
    return pl.pallas_call(
        _ffn_in_kernel,
        out_shape=jax.ShapeDtypeStruct((t, FFN_HIDDEN), BF16),
        grid=(t // tm, per),
        in_specs=[pl.BlockSpec((tm, d), lambda i, j: (i, 0)),
                  pl.BlockSpec((None, d, tn), lambda i, j: (layer, 0, j)),
                  pl.BlockSpec((None, d, tn), lambda i, j: (layer, 0, per + j))],
        out_specs=pl.BlockSpec((tm, tn), lambda i, j: (i, j)),
        compiler_params=_params(("arbitrary", "arbitrary"), blocks, 4 * _nbytes((tm, tn), F32)),
        name="ffn_in",
    )(h, w_ffn_in, w_ffn_in)


def _rope_tables(n_lat_rows, n_ctx_rows):
    half = A_HEAD_DIM // 4
    t = jnp.arange(n_lat_rows)
    freqs = ROPE_THETA ** (-jnp.arange(half, dtype=F32) / half)

    def cos_sin(pos):
        ang = pos.astype(F32)[:, None] * freqs[None, :]
        return jnp.cos(ang), jnp.sin(ang)

    cr, sr = cos_sin(t // GRID_W)
    cc, sc = cos_sin(t % GRID_W)
    cos = jnp.concatenate([cr, cr, cc, cc], axis=-1)
    sin = jnp.concatenate([-sr, sr, -sc, sc], axis=-1)
    cos = jnp.concatenate([cos, jnp.ones((n_ctx_rows, A_HEAD_DIM), F32)], axis=0)
    sin = jnp.concatenate([sin, jnp.zeros((n_ctx_rows, A_HEAD_DIM), F32)], axis=0)
    reps = LANES // A_HEAD_DIM
    return jnp.tile(cos, (1, reps)), jnp.tile(sin, (1, reps))


def kernel(x, c, ctx, c_ctx, w_mod, b_mod, norm1, norm2, w_in, b_gate, a_sink, nb_bias, c_conv_w, c_conv_b,
           c_w_a, c_b_a, c_w_x, c_b_x, c_lam, d_conv_w, w_branch, w_out, w_ffn_in, w_ffn_out, final_norm):
    bsz, s, d = x.shape
    l_ctx = ctx.shape[1]
    assert bsz == 1 and d == D_MODEL and c.shape[0] == 1
    assert s % ROW_CHUNK == 0 and l_ctx % ROW_CHUNK == 0 and s // GRID_W >= NB_ROWS and s >= 3 * A_BLOCK
    t = s + l_ctx
    depth = w_mod.shape[0]

    w_a_cols = w_in[:, :, :A_COLS].astype(BF16)
    w_b_cols = w_in[:, :, A_COLS:A_COLS + B_COLS].astype(BF16)
    w_cd_cols = w_in[:, :, A_COLS + B_COLS:GATE_COL0].astype(BF16)
    w_gate = w_in[:, :, GATE_COL0:].astype(BF16)
    w_branch_b = w_branch.astype(BF16)
    w_ffn_out_b = w_ffn_out.astype(BF16)
    wg_scan = jnp.concatenate([c_w_a, c_w_x], axis=-1).astype(BF16)

    cos_t, sin_t = _rope_tables(s, l_ctx)
    bias_tabs = _nb_bias_tables(nb_bias)

    cs = jnp.concatenate([c, c_ctx[None, :], jnp.zeros((SUBLANES - 2, d), F32)], axis=0)
    mod_all = _modulation(cs, w_mod, b_mod)
    mod_all = mod_all[:, :2].reshape(depth, 2, N_MOD, d).transpose(0, 2, 1, 3)

    xt = jnp.concatenate([x[0], ctx[0]], axis=0)

    tm_big = _pick_tile(t, 1408)
    tm_small = _pick_tile(t, 704)
    h = _norm_mod(xt, norm1[0], mod_all[0], 0, s)
    for l in range(depth):
        mod = mod_all[l]
        za = _matmul(h, w_a_cols, l, tm=tm_big, tn=A_COLS, out_dtype=BF16, name="proj_a", rope=(cos_t, sin_t))
        zb = _matmul(h, w_b_cols, l, tm=tm_big, tn=B_COLS // 2, out_dtype=BF16, name="proj_b")
        zcd = _matmul(h, w_cd_cols, l, tm=tm_big, tn=BRANCH_W, out_dtype=F32, name="proj_cd")
        ya = _attn_a(za, a_sink[l], s)
        yb = _attn_b(zb, bias_tabs, l, s)
        u, yd = _cd_prep(zcd, c_conv_w[l], c_conv_b[l], d_conv_w[l], s)
        hf, hb = _rglru_scan(u, wg_scan[l], c_b_a[l], c_b_x[l], c_lam[l], s)
        yr = _c_final(hf, hb, zcd)
        merged = _merge(h, (ya, yb, yr, yd), w_gate, w_branch_b, b_gate, l, tm=tm_small, tn=256)
        xt, h2 = _resid_norm(merged, w_out, l, xt, mod, 2, norm2[l], mod, 3, s, name="proj_out")
        act = _ffn_in(h2, w_ffn_in, l, tm=tm_big, tn=512)
        if l + 1 < depth:
            xt, h = _resid_norm(act, w_ffn_out_b, l, xt, mod, 5, norm1[l + 1], mod_all[l + 1], 0, s,
                                name="ffn_out")
    out = _resid_final(act, w_ffn_out_b, depth - 1, xt, mod_all[depth - 1], 5, final_norm, s, name="ffn_out_final")
    return out[None]
```

```python
import functools

import numpy as np
import jax
import jax.numpy as jnp
from jax import lax
from jax.experimental import pallas as pl
from jax.experimental.pallas import tpu as pltpu

F32 = jnp.float32
BF16 = jnp.bfloat16

D_MODEL = 2048
DEPTH = 4
GRID_W = 64
EPS = 1e-6
N_MOD = 6
N_BRANCH = 4
BRANCH_W = 512
A_HEADS, A_KV_HEADS, A_HEAD_DIM, A_WINDOW, A_BLOCK = 8, 2, 64, 128, 128
ROPE_THETA = 10000.0
B_HEADS, B_HEAD_DIM, NB_ROWS, NB_COLS = 8, 64, 8, 16
C_BLOCKS, C_BLOCK_W, C_CONV, C_POW = 4, 128, 4, 8.0
D_CONV = 3
FFN_HIDDEN = 5632
A_COLS = (A_HEADS + 2 * A_KV_HEADS) * A_HEAD_DIM
A_ROPE_COLS = (A_HEADS + A_KV_HEADS) * A_HEAD_DIM
B_COLS = 3 * BRANCH_W
CD_COLS = 5 * BRANCH_W
GATE_COL0 = A_COLS + B_COLS + CD_COLS

LANES = 128
SUBLANES = 8
BF16_ROWS = 16
VMEM_BYTES = 64 * 1024 * 1024
VMEM_CAP = VMEM_BYTES - 8 * 1024 * 1024

ROW_CHUNK = 256
ATTN_ROWS = 256


def _pick_tile(total, target, mult=BF16_ROWS):
    best = None
    for d in range(mult, min(total, target) + 1, mult):
        if total % d == 0:
            best = d
    assert best is not None, (total, target, mult)
    return best


def _params(semantics, block_bytes, extra_bytes=0):
    need = 2 * sum(block_bytes) + extra_bytes + (4 << 20)
    return pltpu.CompilerParams(dimension_semantics=semantics,
                                vmem_limit_bytes=int(min(max(need, 16 << 20), VMEM_CAP)))


def _nbytes(shape, dtype):
    return int(np.prod(shape)) * jnp.dtype(dtype).itemsize


def _mod_kernel(cs_ref, w_ref, b_ref, o_ref):
    s = jax.nn.silu(cs_ref[...]).astype(BF16)
    w = w_ref[0].astype(BF16)
    o_ref[0] = jnp.dot(s, w, preferred_element_type=F32) + b_ref[0]


def _modulation(cs, w_mod, b_mod):
    depth, d, n = w_mod.shape
    tn = 1024
    return pl.pallas_call(
        _mod_kernel,
        out_shape=jax.ShapeDtypeStruct((depth, SUBLANES, n), F32),
        grid=(depth, n // tn),
        in_specs=[pl.BlockSpec((SUBLANES, d), lambda l, j: (0, 0)),
                  pl.BlockSpec((1, d, tn), lambda l, j: (l, 0, j)),
                  pl.BlockSpec((1, 1, tn), lambda l, j: (l, 0, j))],
        out_specs=pl.BlockSpec((1, SUBLANES, tn), lambda l, j: (l, 0, j)),
        compiler_params=_params(("arbitrary", "arbitrary"), [_nbytes((d, tn), F32)],
                                _nbytes((d, tn), BF16)),
        name="modulation",
    )(cs, w_mod, b_mod.reshape(depth, 1, n))


def _norm_mod_kernel(x_ref, g_ref, mod_ref, o_ref, *, shift_idx, n_lat_blocks):
    x = x_ref[...]
    y = x * lax.rsqrt(jnp.mean(x * x, axis=-1, keepdims=True) + EPS)
    y = y * g_ref[...]
    is_ctx = pl.program_id(0) >= n_lat_blocks
    shift2 = mod_ref[shift_idx]
    scale2 = mod_ref[shift_idx + 1]
    shift = jnp.where(is_ctx, shift2[1:2], shift2[0:1])
    scale = jnp.where(is_ctx, scale2[1:2], scale2[0:1])
    o_ref[...] = (y * (1 + scale) + shift).astype(o_ref.dtype)


def _norm_mod(x, g, mod, shift_idx, n_lat_rows):
    t, d = x.shape
    return pl.pallas_call(
        functools.partial(_norm_mod_kernel, shift_idx=shift_idx,
                          n_lat_blocks=n_lat_rows // ROW_CHUNK),
        out_shape=jax.ShapeDtypeStruct((t, d), BF16),
        grid=(t // ROW_CHUNK,),
        in_specs=[pl.BlockSpec((ROW_CHUNK, d), lambda i: (i, 0)),
                  pl.BlockSpec((1, d), lambda i: (0, 0)),
                  pl.BlockSpec((N_MOD, 2, d), lambda i: (0, 0, 0))],
        out_specs=pl.BlockSpec((ROW_CHUNK, d), lambda i: (i, 0)),
        compiler_params=_params(("arbitrary",), [_nbytes((ROW_CHUNK, d), F32)] * 2,
                                4 * _nbytes((ROW_CHUNK, d), F32)),
        name="norm_mod",
    )(x, g.reshape(1, d), mod)


def _rope_rotate_half(x):
    lane = lax.broadcasted_iota(jnp.int32, x.shape, 1)
    first = (lane & 16) == 0
    return jnp.where(first, pltpu.roll(x, LANES - 16, 1), pltpu.roll(x, 16, 1))


def _mm_cast_kernel(a_ref, w_ref, o_ref):
    w = w_ref[...].astype(BF16)
    o_ref[...] = jnp.dot(a_ref[...], w, preferred_element_type=F32).astype(o_ref.dtype)


def _mm_rope_kernel(a_ref, w_ref, cos_ref, sin_ref, o_ref):
    acc = jnp.dot(a_ref[...], w_ref[...].astype(BF16), preferred_element_type=F32)
    cos = cos_ref[...]
    sin = sin_ref[...]
    for c in range(A_ROPE_COLS // LANES):
        xc = acc[:, c * LANES:(c + 1) * LANES]
        o_ref[:, c * LANES:(c + 1) * LANES] = (xc * cos + _rope_rotate_half(xc) * sin).astype(o_ref.dtype)
    o_ref[:, A_ROPE_COLS:] = acc[:, A_ROPE_COLS:].astype(o_ref.dtype)


def _matmul(a, w, layer, *, col0, n, tm, tn, out_dtype, name, rope=None):
    t, k = a.shape
    assert col0 % tn == 0 and n % tn == 0 and w.dtype == F32
    cb0 = col0 // tn
    grid = (t // tm, n // tn)
    in_specs = [pl.BlockSpec((tm, k), lambda i, j: (i, 0)),
                pl.BlockSpec((None, k, tn), lambda i, j: (layer, 0, cb0 + j))]
    operands = [a, w]
    blocks = [_nbytes((tm, k), BF16), _nbytes((k, tn), F32), _nbytes((tm, tn), out_dtype)]
    if rope is not None:
        kernel = _mm_rope_kernel
        in_specs += [pl.BlockSpec((tm, LANES), lambda i, j: (i, 0))] * 2
        operands += list(rope)
    else:
        kernel = _mm_cast_kernel
    return pl.pallas_call(
        kernel,
        out_shape=jax.ShapeDtypeStruct((t, n), out_dtype),
        grid=grid,
        in_specs=in_specs,
        out_specs=pl.BlockSpec((tm, tn), lambda i, j: (i, j)),
        compiler_params=_params(("arbitrary", "arbitrary"), blocks,
                                3 * _nbytes((tm, tn), F32) + _nbytes((k, tn), BF16)),
        name=name,
    )(*operands)


CAST_ROWS = 128


def _pick_row(two_rows, is_ctx):
    return jnp.where(is_ctx, two_rows[1:2], two_rows[0:1])


def _rms(x, g):
    y = x * lax.rsqrt(jnp.mean(x * x, axis=-1, keepdims=True) + EPS)
    return y * g


def _resident_bf16(w_ref, scratch):
    if not scratch:
        return w_ref
    wb_ref, = scratch

    @pl.when(pl.program_id(0) == 0)
    def _():
        def body(c, carry):
            r = pl.multiple_of(c * CAST_ROWS, CAST_ROWS)
            wb_ref[pl.ds(r, CAST_ROWS), :] = w_ref[pl.ds(r, CAST_ROWS), :].astype(BF16)
            return carry
        lax.fori_loop(0, w_ref.shape[0] // CAST_ROWS, body, 0)

    return wb_ref


def _resid_norm_kernel(a_ref, w_ref, x_ref, gmod_ref, nmod_ref, g_ref, xo_ref, ho_ref, *scratch,
                       gate_idx, shift_idx, n_lat_blocks):
    w = _resident_bf16(w_ref, scratch)
    is_ctx = pl.program_id(0) >= n_lat_blocks
    acc = jnp.dot(a_ref[...], w[...], preferred_element_type=F32)
    x = x_ref[...] + _pick_row(gmod_ref[gate_idx], is_ctx) * acc
    xo_ref[...] = x
    y = _rms(x, g_ref[...])
    ho_ref[...] = (y * (1 + _pick_row(nmod_ref[shift_idx + 1], is_ctx))
                   + _pick_row(nmod_ref[shift_idx], is_ctx)).astype(ho_ref.dtype)


def _resid_final_kernel(a_ref, w_ref, x_ref, gmod_ref, g_ref, o_ref, *, gate_idx):
    acc = jnp.dot(a_ref[...], w_ref[...], preferred_element_type=F32)
    x = x_ref[...] + gmod_ref[gate_idx][0:1] * acc
    o_ref[...] = _rms(x, g_ref[...])


def _resid_norm(a, w, layer, x, gmod, gate_idx, g, nmod, shift_idx, n_lat_rows, *, name):
    t, k = a.shape
    d = x.shape[1]
    tm = ROW_CHUNK
    cast_w = w.dtype != BF16
    rows = lambda width: pl.BlockSpec((tm, width), lambda i: (i, 0))
    mod_spec = pl.BlockSpec((N_MOD, 2, d), lambda i: (0, 0, 0))
    resident = _nbytes((k, d), w.dtype) + (_nbytes((k, d), BF16) if cast_w else 0)
    return pl.pallas_call(
        functools.partial(_resid_norm_kernel, gate_idx=gate_idx, shift_idx=shift_idx,
                          n_lat_blocks=n_lat_rows // tm),
        out_shape=(jax.ShapeDtypeStruct((t, d), F32), jax.ShapeDtypeStruct((t, d), BF16)),
        grid=(t // tm,),
        in_specs=[rows(k),
                  pl.BlockSpec((None, k, d), lambda i: (layer, 0, 0), pipeline_mode=pl.Buffered(1)),
                  rows(d), mod_spec, mod_spec, pl.BlockSpec((1, d), lambda i: (0, 0))],
        out_specs=(rows(d), rows(d)),
        scratch_shapes=[pltpu.VMEM((k, d), BF16)] if cast_w else [],
        compiler_params=_params(("arbitrary",),
                                [_nbytes((tm, k), BF16), _nbytes((tm, d), F32) * 2, _nbytes((tm, d), BF16)],
                                resident + 4 * _nbytes((tm, d), F32)),
        name=name,
    )(a, w, x, gmod, nmod, g.reshape(1, d))


def _resid_final(a, w, layer, x, gmod, gate_idx, g, n_rows, *, name):
    k = a.shape[1]
    d = x.shape[1]
    tm = ROW_CHUNK
    rows = lambda width: pl.BlockSpec((tm, width), lambda i: (i, 0))
    return pl.pallas_call(
        functools.partial(_resid_final_kernel, gate_idx=gate_idx),
        out_shape=jax.ShapeDtypeStruct((n_rows, d), F32),
        grid=(n_rows // tm,),
        in_specs=[rows(k),
                  pl.BlockSpec((None, k, d), lambda i: (layer, 0, 0), pipeline_mode=pl.Buffered(1)),
                  rows(d), pl.BlockSpec((N_MOD, 2, d), lambda i: (0, 0, 0)),
                  pl.BlockSpec((1, d), lambda i: (0, 0))],
        out_specs=rows(d),
        compiler_params=_params(("arbitrary",), [_nbytes((tm, k), BF16), _nbytes((tm, d), F32) * 2],
                                _nbytes((k, d), BF16) + 4 * _nbytes((tm, d), F32)),
        name=name,
    )(a, w, x, gmod, g.reshape(1, d))


def _dot_nt(a, b):
    return lax.dot_general(a, b, (((1,), (1,)), ((), ())), preferred_element_type=F32)


def _attend(q, kb, vb, kc, vc, bias, sink=None):
    s_loc = _dot_nt(q, kb) + bias
    s_ctx = _dot_nt(q, kc)
    m = jnp.maximum(jnp.max(s_loc, axis=-1, keepdims=True), jnp.max(s_ctx, axis=-1, keepdims=True))
    if sink is not None:
        m = jnp.maximum(m, sink)
    e_loc = jnp.exp(s_loc - m)
    e_ctx = jnp.exp(s_ctx - m)
    den = jnp.sum(e_loc, axis=-1, keepdims=True) + jnp.sum(e_ctx, axis=-1, keepdims=True)
    if sink is not None:
        den = den + jnp.exp(sink - m)
    o = (jnp.dot(e_loc.astype(BF16), vb, preferred_element_type=F32)
         + jnp.dot(e_ctx.astype(BF16), vc, preferred_element_type=F32))
    return o / den


def _attn_a_kernel(sink_ref, q_ref, k_ref, v_ref, o_ref, *, n_lat_rows, n_ctx_rows):
    n = pl.program_id(0)
    rows = q_ref.shape[0]
    band = rows + 2 * A_WINDOW
    is_lat = n * rows < n_lat_rows
    start = pl.multiple_of(jnp.clip(n * rows - A_WINDOW, 0, n_lat_rows - band), A_BLOCK)
    kb = k_ref[pl.ds(start, band), :]
    vb = v_ref[pl.ds(start, band), :]
    kc = k_ref[pl.ds(n_lat_rows, n_ctx_rows), :]
    vc = v_ref[pl.ds(n_lat_rows, n_ctx_rows), :]
    qpos = n * rows + lax.broadcasted_iota(jnp.int32, (rows, band), 0)
    kpos = start + lax.broadcasted_iota(jnp.int32, (rows, band), 1)
    bias = jnp.where((jnp.abs(qpos - kpos) <= A_WINDOW) & is_lat, 0.0, -jnp.inf)
    scale = A_HEAD_DIM ** -0.5
    grp = A_HEADS // A_KV_HEADS
    q = q_ref[...] * scale
    outs = []
    for h in range(A_HEADS):
        kv = slice((h // grp) * A_HEAD_DIM, (h // grp + 1) * A_HEAD_DIM)
        outs.append(_attend(q[:, h * A_HEAD_DIM:(h + 1) * A_HEAD_DIM],
                            kb[:, kv], vb[:, kv], kc[:, kv], vc[:, kv], bias, sink_ref[h]))
    o_ref[...] = jnp.concatenate(outs, axis=1).astype(o_ref.dtype)


def _attn_a(za, sink, n_lat_rows):
    t = za.shape[0]
    qw = A_HEADS * A_HEAD_DIM
    kvw = A_KV_HEADS * A_HEAD_DIM
    assert A_WINDOW == A_BLOCK and n_lat_rows >= ATTN_ROWS + 2 * A_WINDOW
    resident = pl.Buffered(1)
    return pl.pallas_call(
        functools.partial(_attn_a_kernel, n_lat_rows=n_lat_rows, n_ctx_rows=t - n_lat_rows),
        out_shape=jax.ShapeDtypeStruct((t, qw), BF16),
        grid=(t // ATTN_ROWS,),
        in_specs=[pl.BlockSpec(memory_space=pltpu.SMEM),
                  pl.BlockSpec((ATTN_ROWS, qw), lambda n: (n, 0)),
                  pl.BlockSpec((t, kvw), lambda n: (0, qw // kvw), pipeline_mode=resident),
                  pl.BlockSpec((t, kvw), lambda n: (0, qw // kvw + 1), pipeline_mode=resident)],
        out_specs=pl.BlockSpec((ATTN_ROWS, qw), lambda n: (n, 0)),
        compiler_params=_params(("arbitrary",), [_nbytes((t, kvw), BF16)] * 2, 16 << 20),
        name="attn_a",
    )(sink, za, za, za)


B_GROUP = ATTN_ROWS // GRID_W
B_SPAN = 12
B_TABLE_FIRST, B_TABLE_INNER, B_TABLE_LAST, B_TABLE_CTX = range(4)


N_DR = 2 * NB_ROWS - 1
B_LAYOUTS = {B_TABLE_FIRST: (lambda j: 0, NB_ROWS - 1),
             B_TABLE_INNER: (lambda j: j, NB_ROWS - 1 - NB_ROWS // 2),
             B_TABLE_LAST: (lambda j: B_SPAN - NB_ROWS, NB_ROWS - 1 + B_GROUP - B_SPAN)}


def _nb_bias_pairs(rel_bias):
    assert B_GROUP == NB_ROWS // 2 and B_SPAN >= NB_ROWS + B_GROUP - 1 and 2 * GRID_W == LANES
    cq = np.arange(GRID_W)
    cstart = np.clip(cq - NB_COLS // 2, 0, GRID_W - NB_COLS)
    col_ok = (cq[None, :] >= cstart[:, None]) & (cq[None, :] < cstart[:, None] + NB_COLS)
    b = rel_bias.astype(F32)
    pad = GRID_W - NB_COLS
    bpad = jnp.concatenate([jnp.repeat(b[..., :1], pad, axis=-1), b,
                            jnp.repeat(b[..., -1:], pad, axis=-1)], axis=-1)
    off = pad + NB_COLS - 1
    toep = jnp.stack([bpad[..., off - q:off - q + GRID_W] for q in range(GRID_W)], axis=-2)
    toep = jnp.where(col_ok, toep, -jnp.inf)
    hidden = jnp.full_like(toep[..., :1, :, :], -jnp.inf)
    return jnp.concatenate([jnp.concatenate([hidden, toep], axis=-3),
                            jnp.concatenate([toep, hidden], axis=-3)], axis=-1)


def _fill_bias_table(tab_ref, pairs_ref, kind):
    if kind == B_TABLE_CTX:
        tab_ref[...] = jnp.full(tab_ref.shape, -jnp.inf, F32)
        return
    first_row, dr0 = B_LAYOUTS[kind]
    left_lanes = lax.broadcasted_iota(jnp.int32, (GRID_W, LANES), 1) < GRID_W

    def per_head(h, carry):
        for j in range(B_GROUP):
            for p in range(B_SPAN // 2):
                lj = 2 * p
                vis = [first_row(j) <= r < first_row(j) + NB_ROWS for r in (lj, lj + 1)]
                block = pairs_ref[h, lj - j + dr0 + 1] if any(vis) else None
                if vis == [True, False]:
                    block = jnp.where(left_lanes, block, -jnp.inf)
                elif vis == [False, True]:
                    block = jnp.where(left_lanes, -jnp.inf, block)
                elif block is None:
                    block = jnp.full((GRID_W, LANES), -jnp.inf, F32)
                tab_ref[h, j * GRID_W:(j + 1) * GRID_W, p * LANES:(p + 1) * LANES] = block
        return carry

    lax.fori_loop(0, B_HEADS, per_head, 0)


def _attn_b_kernel(q_ref, k_ref, v_ref, pairs_ref, o_ref, tab_ref, *, n_lat_rows, n_ctx_rows):
    g = pl.program_id(0)
    n_lat_groups = n_lat_rows // ATTN_ROWS
    for step, kind in ((0, B_TABLE_FIRST), (1, B_TABLE_INNER), (n_lat_groups - 1, B_TABLE_LAST),
                       (n_lat_groups, B_TABLE_CTX)):
        pl.when(g == step)(functools.partial(_fill_bias_table, tab_ref, pairs_ref, kind))
    grid_rows = n_lat_rows // GRID_W
    win = B_SPAN * GRID_W
    first = jnp.clip(g * B_GROUP - NB_ROWS // 2, 0, grid_rows - B_SPAN)
    start = pl.multiple_of(first * GRID_W, GRID_W)
    kb = k_ref[pl.ds(start, win), :]
    vb = v_ref[pl.ds(start, win), :]
    kc = k_ref[pl.ds(n_lat_rows, n_ctx_rows), :]
    vc = v_ref[pl.ds(n_lat_rows, n_ctx_rows), :]
    q = q_ref[...] * (B_HEAD_DIM ** -0.5)
    outs = []
    for h in range(B_HEADS):
        hs = slice(h * B_HEAD_DIM, (h + 1) * B_HEAD_DIM)
        outs.append(_attend(q[:, hs], kb[:, hs], vb[:, hs], kc[:, hs], vc[:, hs], tab_ref[h]))
    o_ref[...] = jnp.concatenate(outs, axis=1).astype(o_ref.dtype)


def _attn_b(zb, bias_pairs, layer, n_lat_rows):
    t = zb.shape[0]
    w = BRANCH_W
    win = B_SPAN * GRID_W
    assert n_lat_rows // GRID_W >= B_SPAN and n_lat_rows // ATTN_ROWS >= 2
    resident = pl.Buffered(1)
    pairs_block = (None, B_HEADS, N_DR + 1, GRID_W, LANES)
    resident_bytes = (2 * _nbytes((t, w), BF16) + _nbytes(pairs_block[1:], F32)
                      + _nbytes((B_HEADS, ATTN_ROWS, win), F32))
    return pl.pallas_call(
        functools.partial(_attn_b_kernel, n_lat_rows=n_lat_rows, n_ctx_rows=t - n_lat_rows),
        out_shape=jax.ShapeDtypeStruct((t, w), BF16),
        grid=(t // ATTN_ROWS,),
        in_specs=[pl.BlockSpec((ATTN_ROWS, w), lambda g: (g, 0)),
                  pl.BlockSpec((t, w), lambda g: (0, 1), pipeline_mode=resident),
                  pl.BlockSpec((t, w), lambda g: (0, 2), pipeline_mode=resident),
                  pl.BlockSpec(pairs_block, lambda g: (layer, 0, 0, 0, 0), pipeline_mode=resident)],
        out_specs=pl.BlockSpec((ATTN_ROWS, w), lambda g: (g, 0)),
        scratch_shapes=[pltpu.VMEM((B_HEADS, ATTN_ROWS, win), F32)],
        compiler_params=_params(("arbitrary",), [_nbytes((ATTN_ROWS, w), BF16)] * 2,
                                resident_bytes + (16 << 20)),
        name="attn_b",
    )(zb, zb, zb, bias_pairs)


def _cd_prep_kernel(xr_ref, xr_p, xr_n, xd_ref, xd_p, xd_n, cd_ref, cd_p, cd_n, bd_ref,
                    cw_ref, cb_ref, dw_ref, u_ref, yd_ref, *, n_lat_chunks, n_chunks):
    c = pl.program_id(0)
    has_prev = (c != 0) & (c != n_lat_chunks)
    has_next = (c != n_lat_chunks - 1) & (c != n_chunks - 1)
    ch = xr_ref.shape[0]
    h = SUBLANES

    def halo_cat(cur, prev, nxt):
        return jnp.concatenate([jnp.where(has_prev, prev, 0.0), cur, jnp.where(has_next, nxt, 0.0)], axis=0)

    xr = halo_cat(xr_ref[...], xr_p[...], xr_n[...])
    left = C_CONV // 2
    u = xr[h - left:h - left + ch] * cw_ref[0:1, :]
    for j in range(1, C_CONV):
        u = u + xr[h - left + j:h - left + j + ch] * cw_ref[j:j + 1, :]
    u_ref[...] = u + cb_ref[...]

    pd = halo_cat(cd_ref[...] * xd_ref[...], cd_p[...] * xd_p[...], cd_n[...] * xd_n[...])
    left = D_CONV // 2
    y = pd[h - left:h - left + ch] * dw_ref[0:1, :]
    for j in range(1, D_CONV):
        y = y + pd[h - left + j:h - left + j + ch] * dw_ref[j:j + 1, :]
    yd_ref[...] = (bd_ref[...] * y).astype(yd_ref.dtype)


def _cd_prep(zcd, conv_w, conv_b, d_conv_w, n_lat_rows):
    t = zcd.shape[0]
    w = BRANCH_W
    ch = ROW_CHUNK
    n_chunks = t // ch
    per = ch // SUBLANES
    last = t // SUBLANES - 1

    def cur(col):
        return pl.BlockSpec((ch, w), lambda c: (c, col))

    def prev(col):
        return pl.BlockSpec((SUBLANES, w), lambda c: (jnp.maximum(c * per - 1, 0), col))

    def nxt(col):
        return pl.BlockSpec((SUBLANES, w), lambda c: (jnp.minimum((c + 1) * per, last), col))

    small = lambda rows: pl.BlockSpec((rows, w), lambda c: (0, 0))
    return pl.pallas_call(
        functools.partial(_cd_prep_kernel, n_lat_chunks=n_lat_rows // ch, n_chunks=n_chunks),
        out_shape=(jax.ShapeDtypeStruct((t, w), F32), jax.ShapeDtypeStruct((t, w), BF16)),
        grid=(n_chunks,),
        in_specs=[cur(0), prev(0), nxt(0), cur(2), prev(2), nxt(2), cur(4), prev(4), nxt(4), cur(3),
                  small(C_CONV), small(1), small(D_CONV)],
        out_specs=(pl.BlockSpec((ch, w), lambda c: (c, 0)), pl.BlockSpec((ch, w), lambda c: (c, 0))),
        compiler_params=_params(("arbitrary",), [_nbytes((ch, w), F32)] * 6, 8 << 20),
        name="cd_prep",
    )(zcd, zcd, zcd, zcd, zcd, zcd, zcd, zcd, zcd, zcd, conv_w, conv_b.reshape(1, w), d_conv_w)


def _tile_scan(a, b, reverse):
    rows = lax.broadcasted_iota(jnp.int32, a.shape, 0)
    for k in (1, 2, 4):
        if reverse:
            a_sh, b_sh, valid = pltpu.roll(a, SUBLANES - k, 0), pltpu.roll(b, SUBLANES - k, 0), rows < SUBLANES - k
        else:
            a_sh, b_sh, valid = pltpu.roll(a, k, 0), pltpu.roll(b, k, 0), rows >= k
        b = jnp.where(valid, a * b_sh + b, b)
        a = jnp.where(valid, a * a_sh, a)
    return a, b


def _scan_kernel(uf_ref, ub_ref, wg_ref, ba_ref, bx_ref, lam_ref, hf_ref, hb_ref,
                 a_scr, b_scr, carry_f, carry_b):
    @pl.when(pl.program_id(0) == 0)
    def _():
        carry_f[...] = jnp.zeros_like(carry_f)
        carry_b[...] = jnp.zeros_like(carry_b)

    ch, w = uf_ref.shape
    n_tiles = ch // SUBLANES

    def gates(u_ref, d):
        u = u_ref[...]
        ub = u.astype(BF16)
        sp = jax.nn.softplus(-lam_ref[d:d + 1, :])
        for blk in range(C_BLOCKS):
            sl = slice(blk * C_BLOCK_W, (blk + 1) * C_BLOCK_W)
            g = jnp.dot(ub[:, sl], wg_ref[d, blk], preferred_element_type=F32)
            r = jax.nn.sigmoid(g[:, :C_BLOCK_W] + ba_ref[d:d + 1, sl])
            i = jax.nn.sigmoid(g[:, C_BLOCK_W:] + bx_ref[d:d + 1, sl])
            log_a = -C_POW * r * sp[:, sl]
            a = jnp.exp(log_a)
            a_scr[:, sl] = a
            b_scr[:, sl] = jnp.sqrt(-jnp.tanh(log_a) * (a * a + 1.0)) * (i * u[:, sl])

    def scan(out_ref, carry_ref, reverse):
        def body(step, carry):
            tile = n_tiles - 1 - step if reverse else step
            r0 = pl.multiple_of(tile * SUBLANES, SUBLANES)
            a, b = _tile_scan(a_scr[pl.ds(r0, SUBLANES), :], b_scr[pl.ds(r0, SUBLANES), :], reverse)
            h = a * carry + b
            out_ref[pl.ds(r0, SUBLANES), :] = h
            edge = h[0:1, :] if reverse else h[SUBLANES - 1:SUBLANES, :]
            return jnp.broadcast_to(edge, h.shape)

        carry_ref[...] = lax.fori_loop(0, n_tiles, body, carry_ref[...], unroll=4)

    gates(uf_ref, 0)
    scan(hf_ref, carry_f, False)
    gates(ub_ref, 1)
    scan(hb_ref, carry_b, True)


def _rglru_scan(u, wg, b_a, b_x, lam, n_lat_rows):
    t, w = u.shape
    ch = ROW_CHUNK
    n_chunks = t // ch
    n_lat = n_lat_rows // ch
    n_ctx = n_chunks - n_lat

    def fwd_chunk(i):
        return jnp.where(i < n_ctx, n_lat + i, i - n_ctx)

    def bwd_chunk(i):
        return n_chunks - 1 - i

    vec = pl.BlockSpec((2, w), lambda i: (0, 0))
    return pl.pallas_call(
        _scan_kernel,
        out_shape=(jax.ShapeDtypeStruct((t, w), F32), jax.ShapeDtypeStruct((t, w), F32)),
        grid=(n_chunks,),
        in_specs=[pl.BlockSpec((ch, w), lambda i: (fwd_chunk(i), 0)),
                  pl.BlockSpec((ch, w), lambda i: (bwd_chunk(i), 0)),
                  pl.BlockSpec((2, C_BLOCKS, C_BLOCK_W, 2 * C_BLOCK_W), lambda i: (0, 0, 0, 0)),
                  vec, vec, vec],
        out_specs=(pl.BlockSpec((ch, w), lambda i: (fwd_chunk(i), 0)),
                   pl.BlockSpec((ch, w), lambda i: (bwd_chunk(i), 0))),
        scratch_shapes=[pltpu.VMEM((ch, w), F32), pltpu.VMEM((ch, w), F32),
                        pltpu.VMEM((SUBLANES, w), F32), pltpu.VMEM((SUBLANES, w), F32)],
        compiler_params=_params(("arbitrary",), [_nbytes((ch, w), F32)] * 4, 8 << 20),
        name="rglru_scan",
    )(u, u, wg, b_a, b_x, lam)


def _c_final_kernel(hf_ref, hb_ref, gr_ref, o_ref):
    o_ref[...] = ((hf_ref[...] + hb_ref[...]) * jax.nn.gelu(gr_ref[...])).astype(o_ref.dtype)


def _c_final(hf, hb, zcd):
    t, w = hf.shape
    rows = _pick_tile(t, 1024, SUBLANES)
    spec = pl.BlockSpec((rows, w), lambda i: (i, 0))
    return pl.pallas_call(
        _c_final_kernel,
        out_shape=jax.ShapeDtypeStruct((t, w), BF16),
        grid=(t // rows,),
        in_specs=[spec, spec, pl.BlockSpec((rows, w), lambda i: (i, 1))],
        out_specs=spec,
        compiler_params=_params(("arbitrary",), [_nbytes((rows, w), F32)] * 4, 4 << 20),
        name="c_final",
    )(hf, hb, zcd)


def _merge_kernel(h_ref, ya_ref, yb_ref, yr_ref, yd_ref, wg0, wg1, wg2, wg3, wb_ref, bg_ref, o_ref,
                  wg_scr, wb_scr):
    @pl.when(pl.program_id(1) == 0)
    def _():
        for k, wg_ref in enumerate((wg0, wg1, wg2, wg3)):
            wg_scr[k] = wg_ref[...].astype(BF16)
        wb_scr[...] = wb_ref[...].astype(BF16)

    h = h_ref[...]
    merged = None
    for k, y_ref in enumerate((ya_ref, yb_ref, yr_ref, yd_ref)):
        g = jax.nn.sigmoid(jnp.dot(h, wg_scr[k], preferred_element_type=F32) + bg_ref[k:k + 1, :])
        term = g * jnp.dot(y_ref[...], wb_scr[k], preferred_element_type=F32)
        merged = term if merged is None else merged + term
    o_ref[...] = merged.astype(o_ref.dtype)


def _merge(h, ys, w_in, w_branch, b_gate, layer, *, tm, tn):
    t, d = h.shape
    per = d // tn
    assert GATE_COL0 % tn == 0
    gate_block0 = GATE_COL0 // tn
    y_spec = pl.BlockSpec((tm, BRANCH_W), lambda j, i: (i, 0))

    def gate_spec(k):
        return pl.BlockSpec((None, d, tn), lambda j, i: (layer, 0, gate_block0 + k * per + j))

    blocks = ([_nbytes((tm, d), BF16)] + [_nbytes((tm, BRANCH_W), BF16)] * N_BRANCH
              + [_nbytes((d, tn), F32)] * N_BRANCH + [_nbytes((N_BRANCH, BRANCH_W, tn), F32)]
              + [_nbytes((tm, tn), BF16)])
    scratch_bytes = _nbytes((N_BRANCH, d, tn), BF16) + _nbytes((N_BRANCH, BRANCH_W, tn), BF16)
    return pl.pallas_call(
        _merge_kernel,
        out_shape=jax.ShapeDtypeStruct((t, d), BF16),
        grid=(per, t // tm),
        in_specs=[pl.BlockSpec((tm, d), lambda j, i: (i, 0)), y_spec, y_spec, y_spec, y_spec,
                  gate_spec(0), gate_spec(1), gate_spec(2), gate_spec(3),
                  pl.BlockSpec((None, N_BRANCH, BRANCH_W, tn), lambda j, i: (layer, 0, 0, j)),
                  pl.BlockSpec((None, N_BRANCH, tn), lambda j, i: (layer, 0, j))],
        out_specs=pl.BlockSpec((tm, tn), lambda j, i: (i, j)),
        scratch_shapes=[pltpu.VMEM((N_BRANCH, d, tn), BF16), pltpu.VMEM((N_BRANCH, BRANCH_W, tn), BF16)],
        compiler_params=_params(("arbitrary", "arbitrary"), blocks,
                                scratch_bytes + 6 * _nbytes((tm, tn), F32)),
        name="merge",
    )(h, *ys, w_in, w_in, w_in, w_in, w_branch, b_gate)


def _ffn_in_kernel(h_ref, wg_ref, wu_ref, o_ref):
    h = h_ref[...]
    g = jnp.dot(h, wg_ref[...].astype(BF16), preferred_element_type=F32)
    u = jnp.dot(h, wu_ref[...].astype(BF16), preferred_element_type=F32)
    o_ref[...] = (jax.nn.silu(g) * u).astype(o_ref.dtype)


def _ffn_in(h, w_ffn_in, layer, *, tm, tn):
    t, d = h.shape
    per = FFN_HIDDEN // tn
    blocks = [_nbytes((tm, d), BF16), _nbytes((d, tn), F32), _nbytes((d, tn), F32), _nbytes((tm, tn), BF16)]
    return pl.pallas_call(
        _ffn_in_kernel,
        out_shape=jax.ShapeDtypeStruct((t, FFN_HIDDEN), BF16),
        grid=(t // tm, per),
        in_specs=[pl.BlockSpec((tm, d), lambda i, j: (i, 0)),
                  pl.BlockSpec((None, d, tn), lambda i, j: (layer, 0, j)),
                  pl.BlockSpec((None, d, tn), lambda i, j: (layer, 0, per + j))],
        out_specs=pl.BlockSpec((tm, tn), lambda i, j: (i, j)),
        compiler_params=_params(("arbitrary", "arbitrary"), blocks, 4 * _nbytes((tm, tn), F32)),
        name="ffn_in",
    )(h, w_ffn_in, w_ffn_in)


def _rope_tables(n_lat_rows, n_ctx_rows):
    half = A_HEAD_DIM // 4
    t = jnp.arange(n_lat_rows)
    freqs = ROPE_THETA ** (-jnp.arange(half, dtype=F32) / half)

    def cos_sin(pos):
        ang = pos.astype(F32)[:, None] * freqs[None, :]
        return jnp.cos(ang), jnp.sin(ang)

    cr, sr = cos_sin(t // GRID_W)
    cc, sc = cos_sin(t % GRID_W)
    cos = jnp.concatenate([cr, cr, cc, cc], axis=-1)
    sin = jnp.concatenate([-sr, sr, -sc, sc], axis=-1)
    cos = jnp.concatenate([cos, jnp.ones((n_ctx_rows, A_HEAD_DIM), F32)], axis=0)
    sin = jnp.concatenate([sin, jnp.zeros((n_ctx_rows, A_HEAD_DIM), F32)], axis=0)
    reps = LANES // A_HEAD_DIM
    return jnp.tile(cos, (1, reps)), jnp.tile(sin, (1, reps))


def kernel(x, c, ctx, c_ctx, w_mod, b_mod, norm1, norm2, w_in, b_gate, a_sink, nb_bias, c_conv_w, c_conv_b,
           c_w_a, c_b_a, c_w_x, c_b_x, c_lam, d_conv_w, w_branch, w_out, w_ffn_in, w_ffn_out, final_norm):
    bsz, s, d = x.shape
    l_ctx = ctx.shape[1]
    assert bsz == 1 and d == D_MODEL and c.shape[0] == 1
    assert s % ROW_CHUNK == 0 and l_ctx % ROW_CHUNK == 0 and s // GRID_W >= NB_ROWS and s >= 3 * A_BLOCK
    t = s + l_ctx
    depth = w_mod.shape[0]

    w_ffn_out_b = w_ffn_out.astype(BF16)
    wg_scan = jnp.concatenate([c_w_a, c_w_x], axis=-1).astype(BF16)

    cos_t, sin_t = _rope_tables(s, l_ctx)
    bias_pairs = _nb_bias_pairs(nb_bias)

    cs = jnp.concatenate([c, c_ctx[None, :], jnp.zeros((SUBLANES - 2, d), F32)], axis=0)
    mod_all = _modulation(cs, w_mod, b_mod)
    mod_all = mod_all[:, :2].reshape(depth, 2, N_MOD, d).transpose(0, 2, 1, 3)

    xt = jnp.concatenate([x[0], ctx[0]], axis=0)

    tm_big = _pick_tile(t, 1408)
    tm_huge = _pick_tile(t, 2816)
    tm_small = _pick_tile(t, 704)
    h = _norm_mod(xt, norm1[0], mod_all[0], 0, s)
    for l in range(depth):
        mod = mod_all[l]
        za = _matmul(h, w_in, l, col0=0, n=A_COLS, tm=tm_big, tn=A_COLS, out_dtype=BF16, name="proj_a",
                     rope=(cos_t, sin_t))
        zb = _matmul(h, w_in, l, col0=A_COLS, n=B_COLS, tm=tm_big, tn=A_COLS, out_dtype=BF16, name="proj_b")
        zcd = _matmul(h, w_in, l, col0=A_COLS + B_COLS, n=CD_COLS, tm=tm_huge, tn=256, out_dtype=F32,
                      name="proj_cd")
        ya = _attn_a(za, a_sink[l], s)
        yb = _attn_b(zb, bias_pairs, l, s)
        u, yd = _cd_prep(zcd, c_conv_w[l], c_conv_b[l], d_conv_w[l], s)
        hf, hb = _rglru_scan(u, wg_scan[l], c_b_a[l], c_b_x[l], c_lam[l], s)
        yr = _c_final(hf, hb, zcd)
        merged = _merge(h, (ya, yb, yr, yd), w_in, w_branch, b_gate, l, tm=tm_small, tn=256)
        xt, h2 = _resid_norm(merged, w_out, l, xt, mod, 2, norm2[l], mod, 3, s, name="proj_out")
        act = _ffn_in(h2, w_ffn_in, l, tm=tm_big, tn=512)
        if l + 1 < depth:
            xt, h = _resid_norm(act, w_ffn_out_b, l, xt, mod, 5, norm1[l + 1], mod_all[l + 1], 0, s,
                                name="ffn_out")
    out = _resid_final(act, w_ffn_out_b, depth - 1, xt, mod_all[depth - 1], 5, final_norm, s, name="ffn_out_final")
    return out[None]
```

```python
import functools

import numpy as np
import jax
import jax.numpy as jnp
from jax import lax
from jax.experimental import pallas as pl
from jax.experimental.pallas import tpu as pltpu

F32 = jnp.float32
BF16 = jnp.bfloat16

D_MODEL = 2048
DEPTH = 4
GRID_W = 64
EPS = 1e-6
N_MOD = 6
N_BRANCH = 4
BRANCH_W = 512
A_HEADS, A_KV_HEADS, A_HEAD_DIM, A_WINDOW, A_BLOCK = 8, 2, 64, 128, 128
ROPE_THETA = 10000.0
B_HEADS, B_HEAD_DIM, NB_ROWS, NB_COLS = 8, 64, 8, 16
C_BLOCKS, C_BLOCK_W, C_CONV, C_POW = 4, 128, 4, 8.0
D_CONV = 3
FFN_HIDDEN = 5632
A_COLS = (A_HEADS + 2 * A_KV_HEADS) * A_HEAD_DIM
A_ROPE_COLS = (A_HEADS + A_KV_HEADS) * A_HEAD_DIM
B_COLS = 3 * BRANCH_W
CD_COLS = 5 * BRANCH_W
GATE_COL0 = A_COLS + B_COLS + CD_COLS

LANES = 128
SUBLANES = 8
BF16_ROWS = 16
VMEM_BYTES = 64 * 1024 * 1024
VMEM_CAP = VMEM_BYTES - 8 * 1024 * 1024

ROW_CHUNK = 256
ATTN_ROWS = 256


def _pick_tile(total, target, mult=BF16_ROWS):
    best = None
    for d in range(mult, min(total, target) + 1, mult):
        if total % d == 0:
            best = d
    assert best is not None, (total, target, mult)
    return best


def _params(semantics, block_bytes, extra_bytes=0):
    need = 2 * sum(block_bytes) + extra_bytes + (4 << 20)
    return pltpu.CompilerParams(dimension_semantics=semantics,
                                vmem_limit_bytes=int(min(max(need, 16 << 20), VMEM_CAP)))


def _nbytes(shape, dtype):
    return int(np.prod(shape)) * jnp.dtype(dtype).itemsize


def _mod_kernel(cs_ref, w_ref, b_ref, o_ref):
    s = jax.nn.silu(cs_ref[...]).astype(BF16)
    w = w_ref[0].astype(BF16)
    o_ref[0] = jnp.dot(s, w, preferred_element_type=F32) + b_ref[0]


def _modulation(cs, w_mod, b_mod):
    depth, d, n = w_mod.shape
    tn = 1024
    return pl.pallas_call(
        _mod_kernel,
        out_shape=jax.ShapeDtypeStruct((depth, SUBLANES, n), F32),
        grid=(depth, n // tn),
        in_specs=[pl.BlockSpec((SUBLANES, d), lambda l, j: (0, 0)),
                  pl.BlockSpec((1, d, tn), lambda l, j: (l, 0, j)),
                  pl.BlockSpec((1, 1, tn), lambda l, j: (l, 0, j))],
        out_specs=pl.BlockSpec((1, SUBLANES, tn), lambda l, j: (l, 0, j)),
        compiler_params=_params(("arbitrary", "arbitrary"), [_nbytes((d, tn), F32)],
                                _nbytes((d, tn), BF16)),
        name="modulation",
    )(cs, w_mod, b_mod.reshape(depth, 1, n))


def _norm_mod_kernel(x_ref, g_ref, mod_ref, o_ref, *, shift_idx, n_lat_blocks):
    x = x_ref[...]
    y = x * lax.rsqrt(jnp.mean(x * x, axis=-1, keepdims=True) + EPS)
    y = y * g_ref[...]
    is_ctx = pl.program_id(0) >= n_lat_blocks
    shift2 = mod_ref[shift_idx]
    scale2 = mod_ref[shift_idx + 1]
    shift = jnp.where(is_ctx, shift2[1:2], shift2[0:1])
    scale = jnp.where(is_ctx, scale2[1:2], scale2[0:1])
    o_ref[...] = (y * (1 + scale) + shift).astype(o_ref.dtype)


def _norm_mod(x, g, mod, shift_idx, n_lat_rows):
    t, d = x.shape
    return pl.pallas_call(
        functools.partial(_norm_mod_kernel, shift_idx=shift_idx,
                          n_lat_blocks=n_lat_rows // ROW_CHUNK),
        out_shape=jax.ShapeDtypeStruct((t, d), BF16),
        grid=(t // ROW_CHUNK,),
        in_specs=[pl.BlockSpec((ROW_CHUNK, d), lambda i: (i, 0)),
                  pl.BlockSpec((1, d), lambda i: (0, 0)),
                  pl.BlockSpec((N_MOD, 2, d), lambda i: (0, 0, 0))],
        out_specs=pl.BlockSpec((ROW_CHUNK, d), lambda i: (i, 0)),
        compiler_params=_params(("arbitrary",), [_nbytes((ROW_CHUNK, d), F32)] * 2,
                                4 * _nbytes((ROW_CHUNK, d), F32)),
        name="norm_mod",
    )(x, g.reshape(1, d), mod)


def _rope_rotate_half(x):
    lane = lax.broadcasted_iota(jnp.int32, x.shape, 1)
    first = (lane & 16) == 0
    return jnp.where(first, pltpu.roll(x, LANES - 16, 1), pltpu.roll(x, 16, 1))


def _mm_cast_kernel(a_ref, w_ref, o_ref):
    w = w_ref[...].astype(BF16)
    o_ref[...] = jnp.dot(a_ref[...], w, preferred_element_type=F32).astype(o_ref.dtype)


def _mm_rope_kernel(a_ref, w_ref, cos_ref, sin_ref, o_ref):
    acc = jnp.dot(a_ref[...], w_ref[...].astype(BF16), preferred_element_type=F32)
    cos = cos_ref[...]
    sin = sin_ref[...]
    for c in range(A_ROPE_COLS // LANES):
        xc = acc[:, c * LANES:(c + 1) * LANES]
        o_ref[:, c * LANES:(c + 1) * LANES] = (xc * cos + _rope_rotate_half(xc) * sin).astype(o_ref.dtype)
    o_ref[:, A_ROPE_COLS:] = acc[:, A_ROPE_COLS:].astype(o_ref.dtype)


def _matmul(a, w, layer, *, col0, n, tm, tn, out_dtype, name, rope=None):
    t, k = a.shape
    assert col0 % tn == 0 and n % tn == 0 and w.dtype == F32
    cb0 = col0 // tn
    grid = (t // tm, n // tn)
    in_specs = [pl.BlockSpec((tm, k), lambda i, j: (i, 0)),
                pl.BlockSpec((None, k, tn), lambda i, j: (layer, 0, cb0 + j))]
    operands = [a, w]
    blocks = [_nbytes((tm, k), BF16), _nbytes((k, tn), F32), _nbytes((tm, tn), out_dtype)]
    if rope is not None:
        kernel = _mm_rope_kernel
        in_specs += [pl.BlockSpec((tm, LANES), lambda i, j: (i, 0))] * 2
        operands += list(rope)
    else:
        kernel = _mm_cast_kernel
    return pl.pallas_call(
        kernel,
        out_shape=jax.ShapeDtypeStruct((t, n), out_dtype),
        grid=grid,
        in_specs=in_specs,
        out_specs=pl.BlockSpec((tm, tn), lambda i, j: (i, j)),
        compiler_params=_params(("arbitrary", "arbitrary"), blocks,
                                3 * _nbytes((tm, tn), F32) + _nbytes((k, tn), BF16)),
        name=name,
    )(*operands)


CAST_ROWS = 128


def _pick_row(two_rows, is_ctx):
    return jnp.where(is_ctx, two_rows[1:2], two_rows[0:1])


def _rms(x, g):
    y = x * lax.rsqrt(jnp.mean(x * x, axis=-1, keepdims=True) + EPS)
    return y * g


def _resident_bf16(w_ref, scratch):
    if not scratch:
        return w_ref
    wb_ref, = scratch

    @pl.when(pl.program_id(0) == 0)
    def _():
        def body(c, carry):
            r = pl.multiple_of(c * CAST_ROWS, CAST_ROWS)
            wb_ref[pl.ds(r, CAST_ROWS), :] = w_ref[pl.ds(r, CAST_ROWS), :].astype(BF16)
            return carry
        lax.fori_loop(0, w_ref.shape[0] // CAST_ROWS, body, 0)

    return wb_ref


def _resid_norm_kernel(a_ref, w_ref, x_ref, gmod_ref, nmod_ref, g_ref, xo_ref, ho_ref, *scratch,
                       gate_idx, shift_idx, n_lat_blocks):
    w = _resident_bf16(w_ref, scratch)
    is_ctx = pl.program_id(0) >= n_lat_blocks
    gate = _pick_row(gmod_ref[gate_idx], is_ctx)
    scale1 = 1 + _pick_row(nmod_ref[shift_idx + 1], is_ctx)
    shift = _pick_row(nmod_ref[shift_idx], is_ctx)
    for rs in _row_halves(a_ref.shape[0]):
        acc = jnp.dot(a_ref[rs, :], w[...], preferred_element_type=F32)
        x = x_ref[rs, :] + gate * acc
        xo_ref[rs, :] = x
        ho_ref[rs, :] = (_rms(x, g_ref[...]) * scale1 + shift).astype(ho_ref.dtype)


def _resid_final_kernel(a_ref, w_ref, x_ref, gmod_ref, g_ref, o_ref, *, gate_idx):
    for rs in _row_halves(a_ref.shape[0]):
        acc = jnp.dot(a_ref[rs, :], w_ref[...], preferred_element_type=F32)
        x = x_ref[rs, :] + gmod_ref[gate_idx][0:1] * acc
        o_ref[rs, :] = _rms(x, g_ref[...])


def _row_halves(rows):
    half = rows // 2
    return (slice(0, half), slice(half, rows))


def _resident_spec(w, layer):
    k, n = w.shape[-2:]
    if w.ndim == 2:
        return pl.BlockSpec((k, n), lambda i: (0, 0), pipeline_mode=pl.Buffered(1))
    return pl.BlockSpec((None, k, n), lambda i: (layer, 0, 0), pipeline_mode=pl.Buffered(1))


def _resid_norm(a, w, layer, x, gmod, gate_idx, g, nmod, shift_idx, n_lat_rows, *, name):
    t, k = a.shape
    d = x.shape[1]
    tm = ROW_CHUNK
    cast_w = w.dtype != BF16
    rows = lambda width: pl.BlockSpec((tm, width), lambda i: (i, 0))
    mod_spec = pl.BlockSpec((N_MOD, 2, d), lambda i: (0, 0, 0))
    resident = _nbytes((k, d), w.dtype) + (_nbytes((k, d), BF16) if cast_w else 0)
    return pl.pallas_call(
        functools.partial(_resid_norm_kernel, gate_idx=gate_idx, shift_idx=shift_idx,
                          n_lat_blocks=n_lat_rows // tm),
        out_shape=(jax.ShapeDtypeStruct((t, d), F32), jax.ShapeDtypeStruct((t, d), BF16)),
        grid=(t // tm,),
        in_specs=[rows(k),
                  _resident_spec(w, layer),
                  rows(d), mod_spec, mod_spec, pl.BlockSpec((1, d), lambda i: (0, 0))],
        out_specs=(rows(d), rows(d)),
        scratch_shapes=[pltpu.VMEM((k, d), BF16)] if cast_w else [],
        compiler_params=_params(("arbitrary",),
                                [_nbytes((tm, k), BF16), _nbytes((tm, d), F32) * 2, _nbytes((tm, d), BF16)],
                                resident + 4 * _nbytes((tm, d), F32)),
        name=name,
    )(a, w, x, gmod, nmod, g.reshape(1, d))


def _resid_final(a, w, layer, x, gmod, gate_idx, g, n_rows, *, name):
    k = a.shape[1]
    d = x.shape[1]
    tm = ROW_CHUNK
    rows = lambda width: pl.BlockSpec((tm, width), lambda i: (i, 0))
    return pl.pallas_call(
        functools.partial(_resid_final_kernel, gate_idx=gate_idx),
        out_shape=jax.ShapeDtypeStruct((n_rows, d), F32),
        grid=(n_rows // tm,),
        in_specs=[rows(k),
                  _resident_spec(w, layer),
                  rows(d), pl.BlockSpec((N_MOD, 2, d), lambda i: (0, 0, 0)),
                  pl.BlockSpec((1, d), lambda i: (0, 0))],
        out_specs=rows(d),
        compiler_params=_params(("arbitrary",), [_nbytes((tm, k), BF16), _nbytes((tm, d), F32) * 2],
                                _nbytes((k, d), BF16) + 4 * _nbytes((tm, d), F32)),
        name=name,
    )(a, w, x, gmod, g.reshape(1, d))


def _dot_nt(a, b):
    return lax.dot_general(a, b, (((1,), (1,)), ((), ())), preferred_element_type=F32)


def _attend(q, kb, vb, kc, vc, bias, sink=None):
    s_loc = _dot_nt(q, kb) + bias
    s_ctx = _dot_nt(q, kc)
    m = jnp.maximum(jnp.max(s_loc, axis=-1, keepdims=True), jnp.max(s_ctx, axis=-1, keepdims=True))
    if sink is not None:
        m = jnp.maximum(m, sink)
    e_loc = jnp.exp(s_loc - m)
    e_ctx = jnp.exp(s_ctx - m)
    den = jnp.sum(e_loc, axis=-1, keepdims=True) + jnp.sum(e_ctx, axis=-1, keepdims=True)
    if sink is not None:
        den = den + jnp.exp(sink - m)
    o = (jnp.dot(e_loc.astype(BF16), vb, preferred_element_type=F32)
         + jnp.dot(e_ctx.astype(BF16), vc, preferred_element_type=F32))
    return o / den


def _attn_a_kernel(sink_ref, q_ref, k_ref, v_ref, o_ref, *, n_lat_rows, n_ctx_rows):
    n = pl.program_id(0)
    rows = q_ref.shape[0]
    band = rows + 2 * A_WINDOW
    is_lat = n * rows < n_lat_rows
    start = pl.multiple_of(jnp.clip(n * rows - A_WINDOW, 0, n_lat_rows - band), A_BLOCK)
    kb = k_ref[pl.ds(start, band), :]
    vb = v_ref[pl.ds(start, band), :]
    kc = k_ref[pl.ds(n_lat_rows, n_ctx_rows), :]
    vc = v_ref[pl.ds(n_lat_rows, n_ctx_rows), :]
    qpos = n * rows + lax.broadcasted_iota(jnp.int32, (rows, band), 0)
    kpos = start + lax.broadcasted_iota(jnp.int32, (rows, band), 1)
    bias = jnp.where((jnp.abs(qpos - kpos) <= A_WINDOW) & is_lat, 0.0, -jnp.inf)
    scale = A_HEAD_DIM ** -0.5
    grp = A_HEADS // A_KV_HEADS
    q = q_ref[...] * scale
    outs = []
    for h in range(A_HEADS):
        kv = slice((h // grp) * A_HEAD_DIM, (h // grp + 1) * A_HEAD_DIM)
        outs.append(_attend(q[:, h * A_HEAD_DIM:(h + 1) * A_HEAD_DIM],
                            kb[:, kv], vb[:, kv], kc[:, kv], vc[:, kv], bias, sink_ref[h]))
    o_ref[...] = jnp.concatenate(outs, axis=1).astype(o_ref.dtype)


def _attn_a(za, sink, n_lat_rows):
    t = za.shape[0]
    qw = A_HEADS * A_HEAD_DIM
    kvw = A_KV_HEADS * A_HEAD_DIM
    assert A_WINDOW == A_BLOCK and n_lat_rows >= ATTN_ROWS + 2 * A_WINDOW
    resident = pl.Buffered(1)
    return pl.pallas_call(
        functools.partial(_attn_a_kernel, n_lat_rows=n_lat_rows, n_ctx_rows=t - n_lat_rows),
        out_shape=jax.ShapeDtypeStruct((t, qw), BF16),
        grid=(t // ATTN_ROWS,),
        in_specs=[pl.BlockSpec(memory_space=pltpu.SMEM),
                  pl.BlockSpec((ATTN_ROWS, qw), lambda n: (n, 0)),
                  pl.BlockSpec((t, kvw), lambda n: (0, qw // kvw), pipeline_mode=resident),
                  pl.BlockSpec((t, kvw), lambda n: (0, qw // kvw + 1), pipeline_mode=resident)],
        out_specs=pl.BlockSpec((ATTN_ROWS, qw), lambda n: (n, 0)),
        compiler_params=_params(("arbitrary",), [_nbytes((t, kvw), BF16)] * 2, 16 << 20),
        name="attn_a",
    )(sink, za, za, za)


B_GROUP = ATTN_ROWS // GRID_W
B_SPAN = 12
B_TABLE_FIRST, B_TABLE_INNER, B_TABLE_LAST, B_TABLE_CTX = range(4)


N_DR = 2 * NB_ROWS - 1
B_LAYOUTS = {B_TABLE_FIRST: (lambda j: 0, NB_ROWS - 1),
             B_TABLE_INNER: (lambda j: j, NB_ROWS - 1 - NB_ROWS // 2),
             B_TABLE_LAST: (lambda j: B_SPAN - NB_ROWS, NB_ROWS - 1 + B_GROUP - B_SPAN)}


def _nb_bias_rows(rel_bias):
    assert B_GROUP == NB_ROWS // 2 and B_SPAN >= NB_ROWS + B_GROUP - 1 and 2 * GRID_W == LANES
    b = rel_bias.astype(F32)
    pad = GRID_W - NB_COLS
    bpad = jnp.concatenate([jnp.repeat(b[..., :1], pad, axis=-1), b,
                            jnp.repeat(b[..., -1:], pad + 1, axis=-1)], axis=-1)
    off = pad + NB_COLS - 1
    rows = jnp.stack([jnp.roll(bpad, -off, axis=-1), jnp.roll(bpad, GRID_W - off, axis=-1)], axis=-3)
    return jnp.pad(rows, [(0, 0)] * (rows.ndim - 2) + [(0, 1), (0, 0)])


def _fill_bias_table(tab_ref, rows_ref, kind):
    if kind == B_TABLE_CTX:
        tab_ref[...] = jnp.full(tab_ref.shape, -jnp.inf, F32)
        return
    first_row, dr0 = B_LAYOUTS[kind]
    lane = lax.broadcasted_iota(jnp.int32, (GRID_W, LANES), 1)
    qc = lax.broadcasted_iota(jnp.int32, (GRID_W, LANES), 0)
    left_lanes = lane < GRID_W
    kc = lane & (GRID_W - 1)
    cstart = jnp.clip(qc - NB_COLS // 2, 0, GRID_W - NB_COLS)
    col_ok = (kc >= cstart) & (kc < cstart + NB_COLS)

    def per_head(h, carry):
        pairs = {}

        def pair(p):
            if p not in pairs:
                halves = [pltpu.roll(jnp.broadcast_to(rows_ref[h, c, dr:dr + 1, :], (GRID_W, LANES)),
                                     0, 1, stride=1, stride_axis=0)
                          for c, dr in ((0, max(p - 1, 0)), (1, min(p, N_DR - 1)))]
                pairs[p] = jnp.where(col_ok, jnp.where(left_lanes, halves[0], halves[1]), -jnp.inf)
            return pairs[p]

        for j in range(B_GROUP):
            for p in range(B_SPAN // 2):
                lj = 2 * p
                vis = [first_row(j) <= r < first_row(j) + NB_ROWS for r in (lj, lj + 1)]
                block = pair(lj - j + dr0 + 1) if any(vis) else None
                if vis == [True, False]:
                    block = jnp.where(left_lanes, block, -jnp.inf)
                elif vis == [False, True]:
                    block = jnp.where(left_lanes, -jnp.inf, block)
                elif block is None:
                    block = jnp.full((GRID_W, LANES), -jnp.inf, F32)
                tab_ref[h, j * GRID_W:(j + 1) * GRID_W, p * LANES:(p + 1) * LANES] = block
        return carry

    lax.fori_loop(0, B_HEADS, per_head, 0)


def _attn_b_kernel(q_ref, k_ref, v_ref, rows_ref, o_ref, tab_ref, *, n_lat_rows, n_ctx_rows):
    g = pl.program_id(0)
    n_lat_groups = n_lat_rows // ATTN_ROWS
    for step, kind in ((0, B_TABLE_FIRST), (1, B_TABLE_INNER), (n_lat_groups - 1, B_TABLE_LAST),
                       (n_lat_groups, B_TABLE_CTX)):
        pl.when(g == step)(functools.partial(_fill_bias_table, tab_ref, rows_ref, kind))
    grid_rows = n_lat_rows // GRID_W
    win = B_SPAN * GRID_W
    first = jnp.clip(g * B_GROUP - NB_ROWS // 2, 0, grid_rows - B_SPAN)
    start = pl.multiple_of(first * GRID_W, GRID_W)
    kb = k_ref[pl.ds(start, win), :]
    vb = v_ref[pl.ds(start, win), :]
    kc = k_ref[pl.ds(n_lat_rows, n_ctx_rows), :]
    vc = v_ref[pl.ds(n_lat_rows, n_ctx_rows), :]
    q = q_ref[...] * (B_HEAD_DIM ** -0.5)
    outs = []
    for h in range(B_HEADS):
        hs = slice(h * B_HEAD_DIM, (h + 1) * B_HEAD_DIM)
        outs.append(_attend(q[:, hs], kb[:, hs], vb[:, hs], kc[:, hs], vc[:, hs], tab_ref[h]))
    o_ref[...] = jnp.concatenate(outs, axis=1).astype(o_ref.dtype)


def _attn_b(zb, bias_rows, layer, n_lat_rows):
    t = zb.shape[0]
    w = BRANCH_W
    win = B_SPAN * GRID_W
    assert n_lat_rows // GRID_W >= B_SPAN and n_lat_rows // ATTN_ROWS >= 2
    resident = pl.Buffered(1)
    rows_block = (None, B_HEADS, 2, N_DR + 1, LANES)
    resident_bytes = (2 * _nbytes((t, w), BF16) + _nbytes(rows_block[1:], F32)
                      + _nbytes((B_HEADS, ATTN_ROWS, win), F32))
    return pl.pallas_call(
        functools.partial(_attn_b_kernel, n_lat_rows=n_lat_rows, n_ctx_rows=t - n_lat_rows),
        out_shape=jax.ShapeDtypeStruct((t, w), BF16),
        grid=(t // ATTN_ROWS,),
        in_specs=[pl.BlockSpec((ATTN_ROWS, w), lambda g: (g, 0)),
                  pl.BlockSpec((t, w), lambda g: (0, 1), pipeline_mode=resident),
                  pl.BlockSpec((t, w), lambda g: (0, 2), pipeline_mode=resident),
                  pl.BlockSpec(rows_block, lambda g: (layer, 0, 0, 0, 0), pipeline_mode=resident)],
        out_specs=pl.BlockSpec((ATTN_ROWS, w), lambda g: (g, 0)),
        scratch_shapes=[pltpu.VMEM((B_HEADS, ATTN_ROWS, win), F32)],
        compiler_params=_params(("arbitrary",), [_nbytes((ATTN_ROWS, w), BF16)] * 2,
                                resident_bytes + (16 << 20)),
        name="attn_b",
    )(zb, zb, zb, bias_rows)


def _cd_prep_kernel(xr_ref, xr_p, xr_n, xd_ref, xd_p, xd_n, cd_ref, cd_p, cd_n, bd_ref,
                    cw_ref, cb_ref, dw_ref, u_ref, yd_ref, *, n_lat_chunks, n_chunks):
    c = pl.program_id(0)
    has_prev = (c != 0) & (c != n_lat_chunks)
    has_next = (c != n_lat_chunks - 1) & (c != n_chunks - 1)
    ch = xr_ref.shape[0]
    h = SUBLANES

    def halo_cat(cur, prev, nxt):
        return jnp.concatenate([jnp.where(has_prev, prev, 0.0), cur, jnp.where(has_next, nxt, 0.0)], axis=0)

    xr = halo_cat(xr_ref[...], xr_p[...], xr_n[...])
    left = C_CONV // 2
    u = xr[h - left:h - left + ch] * cw_ref[0:1, :]
    for j in range(1, C_CONV):
        u = u + xr[h - left + j:h - left + j + ch] * cw_ref[j:j + 1, :]
    u_ref[...] = u + cb_ref[...]

    pd = halo_cat(cd_ref[...] * xd_ref[...], cd_p[...] * xd_p[...], cd_n[...] * xd_n[...])
    left = D_CONV // 2
    y = pd[h - left:h - left + ch] * dw_ref[0:1, :]
    for j in range(1, D_CONV):
        y = y + pd[h - left + j:h - left + j + ch] * dw_ref[j:j + 1, :]
    yd_ref[...] = (bd_ref[...] * y).astype(yd_ref.dtype)


def _cd_prep(zcd, conv_w, conv_b, d_conv_w, n_lat_rows):
    t = zcd.shape[0]
    w = BRANCH_W
    ch = ROW_CHUNK
    n_chunks = t // ch
    per = ch // SUBLANES
    last = t // SUBLANES - 1

    def cur(col):
        return pl.BlockSpec((ch, w), lambda c: (c, col))

    def prev(col):
        return pl.BlockSpec((SUBLANES, w), lambda c: (jnp.maximum(c * per - 1, 0), col))

    def nxt(col):
        return pl.BlockSpec((SUBLANES, w), lambda c: (jnp.minimum((c + 1) * per, last), col))

    small = lambda rows: pl.BlockSpec((rows, w), lambda c: (0, 0))
    return pl.pallas_call(
        functools.partial(_cd_prep_kernel, n_lat_chunks=n_lat_rows // ch, n_chunks=n_chunks),
        out_shape=(jax.ShapeDtypeStruct((t, w), F32), jax.ShapeDtypeStruct((t, w), BF16)),
        grid=(n_chunks,),
        in_specs=[cur(0), prev(0), nxt(0), cur(2), prev(2), nxt(2), cur(4), prev(4), nxt(4), cur(3),
                  small(C_CONV), small(1), small(D_CONV)],
        out_specs=(pl.BlockSpec((ch, w), lambda c: (c, 0)), pl.BlockSpec((ch, w), lambda c: (c, 0))),
        compiler_params=_params(("arbitrary",), [_nbytes((ch, w), F32)] * 6, 8 << 20),
        name="cd_prep",
    )(zcd, zcd, zcd, zcd, zcd, zcd, zcd, zcd, zcd, zcd, conv_w, conv_b.reshape(1, w), d_conv_w)


def _sigmoid(x):
    return 0.5 * jnp.tanh(0.5 * x) + 0.5


def _tile_scan(a, b, reverse):
    rows = lax.broadcasted_iota(jnp.int32, a.shape, 0)
    for k in (1, 2, 4):
        if reverse:
            a_sh, b_sh, valid = pltpu.roll(a, SUBLANES - k, 0), pltpu.roll(b, SUBLANES - k, 0), rows < SUBLANES - k
        else:
            a_sh, b_sh, valid = pltpu.roll(a, k, 0), pltpu.roll(b, k, 0), rows >= k
        b = jnp.where(valid, a * b_sh + b, b)
        a = jnp.where(valid, a * a_sh, a)
    return a, b


def _scan_kernel(uf_ref, ub_ref, wg_ref, ba_ref, bx_ref, lam_ref, hf_ref, hb_ref,
                 a_scr, b_scr, carry_f, carry_b):
    @pl.when(pl.program_id(0) == 0)
    def _():
        carry_f[...] = jnp.zeros_like(carry_f)
        carry_b[...] = jnp.zeros_like(carry_b)

    ch, w = uf_ref.shape
    n_tiles = ch // SUBLANES

    def gates(u_ref, d):
        u = u_ref[...]
        ub = u.astype(BF16)
        sp = jax.nn.softplus(-lam_ref[d:d + 1, :])
        for blk in range(C_BLOCKS):
            sl = slice(blk * C_BLOCK_W, (blk + 1) * C_BLOCK_W)
            g = jnp.dot(ub[:, sl], wg_ref[d, blk], preferred_element_type=F32)
            r = _sigmoid(g[:, :C_BLOCK_W] + ba_ref[d:d + 1, sl])
            i = _sigmoid(g[:, C_BLOCK_W:] + bx_ref[d:d + 1, sl])
            log_a = -C_POW * r * sp[:, sl]
            a = jnp.exp(log_a)
            a_scr[:, sl] = a
            b_scr[:, sl] = jnp.sqrt(-jnp.tanh(log_a) * (a * a + 1.0)) * (i * u[:, sl])

    def scan(out_ref, carry_ref, reverse):
        def body(step, carry):
            tile = n_tiles - 1 - step if reverse else step
            r0 = pl.multiple_of(tile * SUBLANES, SUBLANES)
            a, b = _tile_scan(a_scr[pl.ds(r0, SUBLANES), :], b_scr[pl.ds(r0, SUBLANES), :], reverse)
            h = a * carry + b
            out_ref[pl.ds(r0, SUBLANES), :] = h
            edge = h[0:1, :] if reverse else h[SUBLANES - 1:SUBLANES, :]
            return jnp.broadcast_to(edge, h.shape)

        carry_ref[...] = lax.fori_loop(0, n_tiles, body, carry_ref[...], unroll=4)

    gates(uf_ref, 0)
    scan(hf_ref, carry_f, False)
    gates(ub_ref, 1)
    scan(hb_ref, carry_b, True)


def _rglru_scan(u, wg, b_a, b_x, lam, n_lat_rows):
    t, w = u.shape
    ch = ROW_CHUNK
    n_chunks = t // ch
    n_lat = n_lat_rows // ch
    n_ctx = n_chunks - n_lat

    def fwd_chunk(i):
        return jnp.where(i < n_ctx, n_lat + i, i - n_ctx)

    def bwd_chunk(i):
        return n_chunks - 1 - i

    vec = pl.BlockSpec((2, w), lambda i: (0, 0))
    return pl.pallas_call(
        _scan_kernel,
        out_shape=(jax.ShapeDtypeStruct((t, w), F32), jax.ShapeDtypeStruct((t, w), F32)),
        grid=(n_chunks,),
        in_specs=[pl.BlockSpec((ch, w), lambda i: (fwd_chunk(i), 0)),
                  pl.BlockSpec((ch, w), lambda i: (bwd_chunk(i), 0)),
                  pl.BlockSpec((2, C_BLOCKS, C_BLOCK_W, 2 * C_BLOCK_W), lambda i: (0, 0, 0, 0)),
                  vec, vec, vec],
        out_specs=(pl.BlockSpec((ch, w), lambda i: (fwd_chunk(i), 0)),
                   pl.BlockSpec((ch, w), lambda i: (bwd_chunk(i), 0))),
        scratch_shapes=[pltpu.VMEM((ch, w), F32), pltpu.VMEM((ch, w), F32),
                        pltpu.VMEM((SUBLANES, w), F32), pltpu.VMEM((SUBLANES, w), F32)],
        compiler_params=_params(("arbitrary",), [_nbytes((ch, w), F32)] * 4, 8 << 20),
        name="rglru_scan",
    )(u, u, wg, b_a, b_x, lam)


def _c_final_kernel(hf_ref, hb_ref, gr_ref, o_ref):
    o_ref[...] = ((hf_ref[...] + hb_ref[...]) * jax.nn.gelu(gr_ref[...])).astype(o_ref.dtype)


def _c_final(hf, hb, zcd):
    t, w = hf.shape
    rows = _pick_tile(t, 1024, SUBLANES)
    spec = pl.BlockSpec((rows, w), lambda i: (i, 0))
    return pl.pallas_call(
        _c_final_kernel,
        out_shape=jax.ShapeDtypeStruct((t, w), BF16),
        grid=(t // rows,),
        in_specs=[spec, spec, pl.BlockSpec((rows, w), lambda i: (i, 1))],
        out_specs=spec,
        compiler_params=_params(("arbitrary",), [_nbytes((rows, w), F32)] * 4, 4 << 20),
        name="c_final",
    )(hf, hb, zcd)


def _merge_kernel(h_ref, ya_ref, yb_ref, yr_ref, yd_ref, wg0, wg1, wg2, wg3, wb_ref, bg_ref, o_ref,
                  wg_scr, wb_scr):
    @pl.when(pl.program_id(1) == 0)
    def _():
        for k, wg_ref in enumerate((wg0, wg1, wg2, wg3)):
            wg_scr[k] = wg_ref[...].astype(BF16)
        wb_scr[...] = wb_ref[...].astype(BF16)

    h = h_ref[...]
    merged = None
    for k, y_ref in enumerate((ya_ref, yb_ref, yr_ref, yd_ref)):
        g = jax.nn.sigmoid(jnp.dot(h, wg_scr[k], preferred_element_type=F32) + bg_ref[k:k + 1, :])
        term = g * jnp.dot(y_ref[...], wb_scr[k], preferred_element_type=F32)
        merged = term if merged is None else merged + term
    o_ref[...] = merged.astype(o_ref.dtype)


def _merge(h, ys, w_in, w_branch, b_gate, layer, *, tm, tn):
    t, d = h.shape
    per = d // tn
    assert GATE_COL0 % tn == 0
    gate_block0 = GATE_COL0 // tn
    y_spec = pl.BlockSpec((tm, BRANCH_W), lambda j, i: (i, 0))

    def gate_spec(k):
        return pl.BlockSpec((None, d, tn), lambda j, i: (layer, 0, gate_block0 + k * per + j))

    blocks = ([_nbytes((tm, d), BF16)] + [_nbytes((tm, BRANCH_W), BF16)] * N_BRANCH
              + [_nbytes((d, tn), F32)] * N_BRANCH + [_nbytes((N_BRANCH, BRANCH_W, tn), F32)]
              + [_nbytes((tm, tn), BF16)])
    scratch_bytes = _nbytes((N_BRANCH, d, tn), BF16) + _nbytes((N_BRANCH, BRANCH_W, tn), BF16)
    return pl.pallas_call(
        _merge_kernel,
        out_shape=jax.ShapeDtypeStruct((t, d), BF16),
        grid=(per, t // tm),
        in_specs=[pl.BlockSpec((tm, d), lambda j, i: (i, 0)), y_spec, y_spec, y_spec, y_spec,
                  gate_spec(0), gate_spec(1), gate_spec(2), gate_spec(3),
                  pl.BlockSpec((None, N_BRANCH, BRANCH_W, tn), lambda j, i: (layer, 0, 0, j)),
                  pl.BlockSpec((None, N_BRANCH, tn), lambda j, i: (layer, 0, j))],
        out_specs=pl.BlockSpec((tm, tn), lambda j, i: (i, j)),
        scratch_shapes=[pltpu.VMEM((N_BRANCH, d, tn), BF16), pltpu.VMEM((N_BRANCH, BRANCH_W, tn), BF16)],
        compiler_params=_params(("arbitrary", "arbitrary"), blocks,
                                scratch_bytes + 6 * _nbytes((tm, tn), F32)),
        name="merge",
    )(h, *ys, w_in, w_in, w_in, w_in, w_branch, b_gate)


def _ffn_in_kernel(h_ref, wg_ref, wu_ref, wo_ref, o_ref, wo_bf_ref, *, n_cast_steps, per):
    h = h_ref[...]
    g = jnp.dot(h, wg_ref[...].astype(BF16), preferred_element_type=F32)
    u = jnp.dot(h, wu_ref[...].astype(BF16), preferred_element_type=F32)
    o_ref[...] = (jax.nn.silu(g) * u).astype(o_ref.dtype)

    @pl.when(pl.program_id(0) * per + pl.program_id(1) < n_cast_steps)
    def _():
        wo_bf_ref[...] = wo_ref[...].astype(BF16)


def _ffn_in(h, w_ffn_in, w_ffn_out, layer, *, tm, tn):
    t, d = h.shape
    per = FFN_HIDDEN // tn
    n_steps = (t // tm) * per
    cast_rows = min(r for r in range(BF16_ROWS, FFN_HIDDEN + 1, BF16_ROWS)
                    if FFN_HIDDEN % r == 0 and FFN_HIDDEN // r <= n_steps)
    n_cast_steps = FFN_HIDDEN // cast_rows

    def cast_block(i, j):
        return (jnp.minimum(i * per + j, n_cast_steps - 1), 0)

    blocks = [_nbytes((tm, d), BF16), _nbytes((d, tn), F32), _nbytes((d, tn), F32), _nbytes((tm, tn), BF16),
              _nbytes((cast_rows, d), F32), _nbytes((cast_rows, d), BF16)]
    return pl.pallas_call(
        functools.partial(_ffn_in_kernel, n_cast_steps=n_cast_steps, per=per),
        out_shape=(jax.ShapeDtypeStruct((t, FFN_HIDDEN), BF16),
                   jax.ShapeDtypeStruct((FFN_HIDDEN, d), BF16)),
        grid=(t // tm, per),
        in_specs=[pl.BlockSpec((tm, d), lambda i, j: (i, 0)),
                  pl.BlockSpec((None, d, tn), lambda i, j: (layer, 0, j)),
                  pl.BlockSpec((None, d, tn), lambda i, j: (layer, 0, per + j)),
                  pl.BlockSpec((None, cast_rows, d), lambda i, j: (layer,) + cast_block(i, j))],
        out_specs=(pl.BlockSpec((tm, tn), lambda i, j: (i, j)),
                   pl.BlockSpec((cast_rows, d), cast_block)),
        compiler_params=_params(("arbitrary", "arbitrary"), blocks, 4 * _nbytes((tm, tn), F32)),
        name="ffn_in",
    )(h, w_ffn_in, w_ffn_in, w_ffn_out)


def _rope_tables(n_lat_rows, n_ctx_rows):
    half = A_HEAD_DIM // 4
    t = jnp.arange(n_lat_rows)
    freqs = ROPE_THETA ** (-jnp.arange(half, dtype=F32) / half)

    def cos_sin(pos):
        ang = pos.astype(F32)[:, None] * freqs[None, :]
        return jnp.cos(ang), jnp.sin(ang)

    cr, sr = cos_sin(t // GRID_W)
    cc, sc = cos_sin(t % GRID_W)
    cos = jnp.concatenate([cr, cr, cc, cc], axis=-1)
    sin = jnp.concatenate([-sr, sr, -sc, sc], axis=-1)
    cos = jnp.concatenate([cos, jnp.ones((n_ctx_rows, A_HEAD_DIM), F32)], axis=0)
    sin = jnp.concatenate([sin, jnp.zeros((n_ctx_rows, A_HEAD_DIM), F32)], axis=0)
    reps = LANES // A_HEAD_DIM
    return jnp.tile(cos, (1, reps)), jnp.tile(sin, (1, reps))


def kernel(x, c, ctx, c_ctx, w_mod, b_mod, norm1, norm2, w_in, b_gate, a_sink, nb_bias, c_conv_w, c_conv_b,
           c_w_a, c_b_a, c_w_x, c_b_x, c_lam, d_conv_w, w_branch, w_out, w_ffn_in, w_ffn_out, final_norm):
    bsz, s, d = x.shape
    l_ctx = ctx.shape[1]
    assert bsz == 1 and d == D_MODEL and c.shape[0] == 1
    assert s % ROW_CHUNK == 0 and l_ctx % ROW_CHUNK == 0 and s // GRID_W >= NB_ROWS and s >= 3 * A_BLOCK
    t = s + l_ctx
    depth = w_mod.shape[0]

    wg_scan = jnp.concatenate([c_w_a, c_w_x], axis=-1).astype(BF16)

    cos_t, sin_t = _rope_tables(s, l_ctx)
    bias_rows = _nb_bias_rows(nb_bias)

    cs = jnp.concatenate([c, c_ctx[None, :], jnp.zeros((SUBLANES - 2, d), F32)], axis=0)
    mod_all = _modulation(cs, w_mod, b_mod)
    mod_all = mod_all[:, :2].reshape(depth, 2, N_MOD, d).transpose(0, 2, 1, 3)

    xt = jnp.concatenate([x[0], ctx[0]], axis=0)

    tm_big = _pick_tile(t, 1408)
    tm_huge = _pick_tile(t, 2816)
    tm_small = _pick_tile(t, 704)
    h = _norm_mod(xt, norm1[0], mod_all[0], 0, s)
    for l in range(depth):
        mod = mod_all[l]
        za = _matmul(h, w_in, l, col0=0, n=A_COLS, tm=tm_big, tn=A_COLS, out_dtype=BF16, name="proj_a",
                     rope=(cos_t, sin_t))
        zb = _matmul(h, w_in, l, col0=A_COLS, n=B_COLS, tm=tm_big, tn=A_COLS, out_dtype=BF16, name="proj_b")
        zcd = _matmul(h, w_in, l, col0=A_COLS + B_COLS, n=CD_COLS, tm=tm_huge, tn=256, out_dtype=F32,
                      name="proj_cd")
        ya = _attn_a(za, a_sink[l], s)
        yb = _attn_b(zb, bias_rows, l, s)
        u, yd = _cd_prep(zcd, c_conv_w[l], c_conv_b[l], d_conv_w[l], s)
        hf, hb = _rglru_scan(u, wg_scan[l], c_b_a[l], c_b_x[l], c_lam[l], s)
        yr = _c_final(hf, hb, zcd)
        merged = _merge(h, (ya, yb, yr, yd), w_in, w_branch, b_gate, l, tm=tm_small, tn=256)
        xt, h2 = _resid_norm(merged, w_out, l, xt, mod, 2, norm2[l], mod, 3, s, name="proj_out")
        act, w_ffn_out_b = _ffn_in(h2, w_ffn_in, w_ffn_out, l, tm=tm_big, tn=512)
        if l + 1 < depth:
            xt, h = _resid_norm(act, w_ffn_out_b, None, xt, mod, 5, norm1[l + 1], mod_all[l + 1], 0, s,
                                name="ffn_out")
    out = _resid_final(act, w_ffn_out_b, None, xt, mod_all[depth - 1], 5, final_norm, s, name="ffn_out_final")
    return out[None]
```

```python
import functools

import numpy as np
import jax
import jax.numpy as jnp
from jax import lax
from jax.experimental import pallas as pl
from jax.experimental.pallas import tpu as pltpu

F32 = jnp.float32
BF16 = jnp.bfloat16

D_MODEL = 2048
DEPTH = 4
GRID_W = 64
EPS = 1e-6
N_MOD = 6
N_BRANCH = 4
BRANCH_W = 512
A_HEADS, A_KV_HEADS, A_HEAD_DIM, A_WINDOW, A_BLOCK = 8, 2, 64, 128, 128
ROPE_THETA = 10000.0
B_HEADS, B_HEAD_DIM, NB_ROWS, NB_COLS = 8, 64, 8, 16
C_BLOCKS, C_BLOCK_W, C_CONV, C_POW = 4, 128, 4, 8.0
D_CONV = 3
FFN_HIDDEN = 5632
A_COLS = (A_HEADS + 2 * A_KV_HEADS) * A_HEAD_DIM
A_ROPE_COLS = (A_HEADS + A_KV_HEADS) * A_HEAD_DIM
B_COLS = 3 * BRANCH_W
CD_COLS = 5 * BRANCH_W
GATE_COL0 = A_COLS + B_COLS + CD_COLS

LANES = 128
SUBLANES = 8
BF16_ROWS = 16
VMEM_BYTES = 64 * 1024 * 1024
VMEM_CAP = VMEM_BYTES - 8 * 1024 * 1024

ROW_CHUNK = 256
ATTN_ROWS = 256


def _pick_tile(total, target, mult=BF16_ROWS):
    best = None
    for d in range(mult, min(total, target) + 1, mult):
        if total % d == 0:
            best = d
    assert best is not None, (total, target, mult)
    return best


def _params(semantics, block_bytes, extra_bytes=0):
    need = 2 * sum(block_bytes) + extra_bytes + (4 << 20)
    return pltpu.CompilerParams(dimension_semantics=semantics,
                                vmem_limit_bytes=int(min(max(need, 16 << 20), VMEM_CAP)))


def _nbytes(shape, dtype):
    return int(np.prod(shape)) * jnp.dtype(dtype).itemsize


def _mod_kernel(cs_ref, w_ref, b_ref, o_ref):
    s = jax.nn.silu(cs_ref[...]).astype(BF16)
    w = w_ref[0].astype(BF16)
    o_ref[0] = jnp.dot(s, w, preferred_element_type=F32) + b_ref[0]


def _modulation(cs, w_mod, b_mod):
    depth, d, n = w_mod.shape
    tn = 1024
    return pl.pallas_call(
        _mod_kernel,
        out_shape=jax.ShapeDtypeStruct((depth, SUBLANES, n), F32),
        grid=(depth, n // tn),
        in_specs=[pl.BlockSpec((SUBLANES, d), lambda l, j: (0, 0)),
                  pl.BlockSpec((1, d, tn), lambda l, j: (l, 0, j)),
                  pl.BlockSpec((1, 1, tn), lambda l, j: (l, 0, j))],
        out_specs=pl.BlockSpec((1, SUBLANES, tn), lambda l, j: (l, 0, j)),
        compiler_params=_params(("arbitrary", "arbitrary"), [_nbytes((d, tn), F32)],
                                _nbytes((d, tn), BF16)),
        name="modulation",
    )(cs, w_mod, b_mod.reshape(depth, 1, n))


def _norm_mod_kernel(x_ref, g_ref, mod_ref, o_ref, *, shift_idx, n_lat_blocks):
    x = x_ref[...]
    y = x * lax.rsqrt(jnp.mean(x * x, axis=-1, keepdims=True) + EPS)
    y = y * g_ref[...]
    is_ctx = pl.program_id(0) >= n_lat_blocks
    shift2 = mod_ref[shift_idx]
    scale2 = mod_ref[shift_idx + 1]
    shift = jnp.where(is_ctx, shift2[1:2], shift2[0:1])
    scale = jnp.where(is_ctx, scale2[1:2], scale2[0:1])
    o_ref[...] = (y * (1 + scale) + shift).astype(o_ref.dtype)


def _norm_mod(x, g, mod, shift_idx, n_lat_rows):
    t, d = x.shape
    return pl.pallas_call(
        functools.partial(_norm_mod_kernel, shift_idx=shift_idx,
                          n_lat_blocks=n_lat_rows // ROW_CHUNK),
        out_shape=jax.ShapeDtypeStruct((t, d), BF16),
        grid=(t // ROW_CHUNK,),
        in_specs=[pl.BlockSpec((ROW_CHUNK, d), lambda i: (i, 0)),
                  pl.BlockSpec((1, d), lambda i: (0, 0)),
                  pl.BlockSpec((N_MOD, 2, d), lambda i: (0, 0, 0))],
        out_specs=pl.BlockSpec((ROW_CHUNK, d), lambda i: (i, 0)),
        compiler_params=_params(("arbitrary",), [_nbytes((ROW_CHUNK, d), F32)] * 2,
                                4 * _nbytes((ROW_CHUNK, d), F32)),
        name="norm_mod",
    )(x, g.reshape(1, d), mod)


def _rope_rotate_half(x):
    lane = lax.broadcasted_iota(jnp.int32, x.shape, 1)
    first = (lane & 16) == 0
    return jnp.where(first, pltpu.roll(x, LANES - 16, 1), pltpu.roll(x, 16, 1))


def _mm_cast_kernel(a_ref, w_ref, o_ref):
    w = w_ref[...].astype(BF16)
    o_ref[...] = jnp.dot(a_ref[...], w, preferred_element_type=F32).astype(o_ref.dtype)


def _mm_rope_kernel(a_ref, w_ref, cos_ref, sin_ref, o_ref):
    acc = jnp.dot(a_ref[...], w_ref[...].astype(BF16), preferred_element_type=F32)
    cos = cos_ref[...]
    sin = sin_ref[...]
    for c in range(A_ROPE_COLS // LANES):
        xc = acc[:, c * LANES:(c + 1) * LANES]
        o_ref[:, c * LANES:(c + 1) * LANES] = (xc * cos + _rope_rotate_half(xc) * sin).astype(o_ref.dtype)
    o_ref[:, A_ROPE_COLS:] = acc[:, A_ROPE_COLS:].astype(o_ref.dtype)


def _matmul(a, w, layer, *, col0, n, tm, tn, out_dtype, name, rope=None):
    t, k = a.shape
    assert col0 % tn == 0 and n % tn == 0 and w.dtype == F32
    cb0 = col0 // tn
    grid = (t // tm, n // tn)
    in_specs = [pl.BlockSpec((tm, k), lambda i, j: (i, 0)),
                pl.BlockSpec((None, k, tn), lambda i, j: (layer, 0, cb0 + j))]
    operands = [a, w]
    blocks = [_nbytes((tm, k), BF16), _nbytes((k, tn), F32), _nbytes((tm, tn), out_dtype)]
    if rope is not None:
        kernel = _mm_rope_kernel
        in_specs += [pl.BlockSpec((tm, LANES), lambda i, j: (i, 0))] * 2
        operands += list(rope)
    else:
        kernel = _mm_cast_kernel
    return pl.pallas_call(
        kernel,
        out_shape=jax.ShapeDtypeStruct((t, n), out_dtype),
        grid=grid,
        in_specs=in_specs,
        out_specs=pl.BlockSpec((tm, tn), lambda i, j: (i, j)),
        compiler_params=_params(("arbitrary", "arbitrary"), blocks,
                                3 * _nbytes((tm, tn), F32) + _nbytes((k, tn), BF16)),
        name=name,
    )(*operands)


CAST_ROWS = 128


def _pick_row(two_rows, is_ctx):
    return jnp.where(is_ctx, two_rows[1:2], two_rows[0:1])


def _rms(x, g):
    y = x * lax.rsqrt(jnp.mean(x * x, axis=-1, keepdims=True) + EPS)
    return y * g


def _resident_bf16(w_ref, scratch):
    if not scratch:
        return w_ref
    wb_ref, = scratch

    @pl.when(pl.program_id(0) == 0)
    def _():
        def body(c, carry):
            r = pl.multiple_of(c * CAST_ROWS, CAST_ROWS)
            wb_ref[pl.ds(r, CAST_ROWS), :] = w_ref[pl.ds(r, CAST_ROWS), :].astype(BF16)
            return carry
        lax.fori_loop(0, w_ref.shape[0] // CAST_ROWS, body, 0)

    return wb_ref


def _resid_norm_kernel(a_ref, w_ref, x_ref, gmod_ref, nmod_ref, g_ref, xo_ref, ho_ref, *scratch,
                       gate_idx, shift_idx, n_lat_blocks):
    w = _resident_bf16(w_ref, scratch)
    is_ctx = pl.program_id(0) >= n_lat_blocks
    gate = _pick_row(gmod_ref[gate_idx], is_ctx)
    scale1 = 1 + _pick_row(nmod_ref[shift_idx + 1], is_ctx)
    shift = _pick_row(nmod_ref[shift_idx], is_ctx)
    acc = jnp.dot(a_ref[...], w[...], preferred_element_type=F32)
    x = x_ref[...] + gate * acc
    xo_ref[...] = x
    ho_ref[...] = (_rms(x, g_ref[...]) * scale1 + shift).astype(ho_ref.dtype)


def _resid_final_kernel(a_ref, w_ref, x_ref, gmod_ref, g_ref, o_ref, *, gate_idx):
    acc = jnp.dot(a_ref[...], w_ref[...], preferred_element_type=F32)
    x = x_ref[...] + gmod_ref[gate_idx][0:1] * acc
    o_ref[...] = _rms(x, g_ref[...])


def _resident_spec(w, layer):
    k, n = w.shape[-2:]
    if w.ndim == 2:
        return pl.BlockSpec((k, n), lambda i: (0, 0), pipeline_mode=pl.Buffered(1))
    return pl.BlockSpec((None, k, n), lambda i: (layer, 0, 0), pipeline_mode=pl.Buffered(1))


def _resid_norm(a, w, layer, x, gmod, gate_idx, g, nmod, shift_idx, n_lat_rows, *, name):
    t, k = a.shape
    d = x.shape[1]
    tm = ROW_CHUNK
    cast_w = w.dtype != BF16
    rows = lambda width: pl.BlockSpec((tm, width), lambda i: (i, 0))
    mod_spec = pl.BlockSpec((N_MOD, 2, d), lambda i: (0, 0, 0))
    resident = _nbytes((k, d), w.dtype) + (_nbytes((k, d), BF16) if cast_w else 0)
    return pl.pallas_call(
        functools.partial(_resid_norm_kernel, gate_idx=gate_idx, shift_idx=shift_idx,
                          n_lat_blocks=n_lat_rows // tm),
        out_shape=(jax.ShapeDtypeStruct((t, d), F32), jax.ShapeDtypeStruct((t, d), BF16)),
        grid=(t // tm,),
        in_specs=[rows(k),
                  _resident_spec(w, layer),
                  rows(d), mod_spec, mod_spec, pl.BlockSpec((1, d), lambda i: (0, 0))],
        out_specs=(rows(d), rows(d)),
        scratch_shapes=[pltpu.VMEM((k, d), BF16)] if cast_w else [],
        compiler_params=_params(("arbitrary",),
                                [_nbytes((tm, k), BF16), _nbytes((tm, d), F32) * 2, _nbytes((tm, d), BF16)],
                                resident + 4 * _nbytes((tm, d), F32)),
        name=name,
    )(a, w, x, gmod, nmod, g.reshape(1, d))


def _resid_final(a, w, layer, x, gmod, gate_idx, g, n_rows, *, name):
    k = a.shape[1]
    d = x.shape[1]
    tm = ROW_CHUNK
    rows = lambda width: pl.BlockSpec((tm, width), lambda i: (i, 0))
    return pl.pallas_call(
        functools.partial(_resid_final_kernel, gate_idx=gate_idx),
        out_shape=jax.ShapeDtypeStruct((n_rows, d), F32),
        grid=(n_rows // tm,),
        in_specs=[rows(k),
                  _resident_spec(w, layer),
                  rows(d), pl.BlockSpec((N_MOD, 2, d), lambda i: (0, 0, 0)),
                  pl.BlockSpec((1, d), lambda i: (0, 0))],
        out_specs=rows(d),
        compiler_params=_params(("arbitrary",), [_nbytes((tm, k), BF16), _nbytes((tm, d), F32) * 2],
                                _nbytes((k, d), BF16) + 4 * _nbytes((tm, d), F32)),
        name=name,
    )(a, w, x, gmod, g.reshape(1, d))


def _dot_nt(a, b):
    return lax.dot_general(a, b, (((1,), (1,)), ((), ())), preferred_element_type=F32)


def _attend(q, kb, vb, kc, vc, bias, sink=None):
    s_loc = _dot_nt(q, kb) + bias
    s_ctx = _dot_nt(q, kc)
    m = jnp.maximum(jnp.max(s_loc, axis=-1, keepdims=True), jnp.max(s_ctx, axis=-1, keepdims=True))
    if sink is not None:
        m = jnp.maximum(m, sink)
    e_loc = jnp.exp(s_loc - m)
    e_ctx = jnp.exp(s_ctx - m)
    den = jnp.sum(e_loc, axis=-1, keepdims=True) + jnp.sum(e_ctx, axis=-1, keepdims=True)
    if sink is not None:
        den = den + jnp.exp(sink - m)
    o = (jnp.dot(e_loc.astype(BF16), vb, preferred_element_type=F32)
         + jnp.dot(e_ctx.astype(BF16), vc, preferred_element_type=F32))
    return o / den


def _attn_a_kernel(sink_ref, q_ref, k_ref, v_ref, o_ref, *, n_lat_rows, n_ctx_rows):
    n = pl.program_id(0)
    rows = q_ref.shape[0]
    band = rows + 2 * A_WINDOW
    is_lat = n * rows < n_lat_rows
    start = pl.multiple_of(jnp.clip(n * rows - A_WINDOW, 0, n_lat_rows - band), A_BLOCK)
    keys = jnp.concatenate([k_ref[pl.ds(start, band), :], k_ref[pl.ds(n_lat_rows, n_ctx_rows), :]], axis=0)
    vals = jnp.concatenate([v_ref[pl.ds(start, band), :], v_ref[pl.ds(n_lat_rows, n_ctx_rows), :]], axis=0)
    n_keys = band + n_ctx_rows
    qpos = n * rows + lax.broadcasted_iota(jnp.int32, (rows, n_keys), 0)
    col = lax.broadcasted_iota(jnp.int32, (rows, n_keys), 1)
    visible = (col >= band) | ((jnp.abs(qpos - (start + col)) <= A_WINDOW) & is_lat)
    bias = jnp.where(visible, 0.0, -jnp.inf)
    scale = A_HEAD_DIM ** -0.5
    grp = A_HEADS // A_KV_HEADS
    q = q_ref[...] * scale
    outs = []
    for kvh in range(A_KV_HEADS):
        kv = slice(kvh * A_HEAD_DIM, (kvh + 1) * A_HEAD_DIM)
        heads = range(kvh * grp, (kvh + 1) * grp)
        qs = jnp.concatenate([q[:, h * A_HEAD_DIM:(h + 1) * A_HEAD_DIM] for h in heads], axis=0)
        s_all = _dot_nt(qs, keys[:, kv])
        es, dens = [], []
        for gi, h in enumerate(heads):
            s = s_all[gi * rows:(gi + 1) * rows] + bias
            m = jnp.maximum(jnp.max(s, axis=-1, keepdims=True), sink_ref[h])
            e = jnp.exp(s - m)
            dens.append(jnp.sum(e, axis=-1, keepdims=True) + jnp.exp(sink_ref[h] - m))
            es.append(e.astype(BF16))
        o_all = jnp.dot(jnp.concatenate(es, axis=0), vals[:, kv], preferred_element_type=F32)
        outs += [o_all[gi * rows:(gi + 1) * rows] / dens[gi] for gi in range(grp)]
    o_ref[...] = jnp.concatenate(outs, axis=1).astype(o_ref.dtype)


def _attn_a(za, sink, n_lat_rows):
    t = za.shape[0]
    qw = A_HEADS * A_HEAD_DIM
    kvw = A_KV_HEADS * A_HEAD_DIM
    assert A_WINDOW == A_BLOCK and n_lat_rows >= ATTN_ROWS + 2 * A_WINDOW
    resident = pl.Buffered(1)
    return pl.pallas_call(
        functools.partial(_attn_a_kernel, n_lat_rows=n_lat_rows, n_ctx_rows=t - n_lat_rows),
        out_shape=jax.ShapeDtypeStruct((t, qw), BF16),
        grid=(t // ATTN_ROWS,),
        in_specs=[pl.BlockSpec(memory_space=pltpu.SMEM),
                  pl.BlockSpec((ATTN_ROWS, qw), lambda n: (n, 0)),
                  pl.BlockSpec((t, kvw), lambda n: (0, qw // kvw), pipeline_mode=resident),
                  pl.BlockSpec((t, kvw), lambda n: (0, qw // kvw + 1), pipeline_mode=resident)],
        out_specs=pl.BlockSpec((ATTN_ROWS, qw), lambda n: (n, 0)),
        compiler_params=_params(("arbitrary",), [_nbytes((t, kvw), BF16)] * 2, 16 << 20),
        name="attn_a",
    )(sink, za, za, za)


B_GROUP = ATTN_ROWS // GRID_W
B_SPAN = 12
B_TABLE_FIRST, B_TABLE_INNER, B_TABLE_LAST, B_TABLE_CTX = range(4)


N_DR = 2 * NB_ROWS - 1
B_LAYOUTS = {B_TABLE_FIRST: (lambda j: 0, NB_ROWS - 1),
             B_TABLE_INNER: (lambda j: j, NB_ROWS - 1 - NB_ROWS // 2),
             B_TABLE_LAST: (lambda j: B_SPAN - NB_ROWS, NB_ROWS - 1 + B_GROUP - B_SPAN)}


def _nb_bias_rows(rel_bias):
    assert B_GROUP == NB_ROWS // 2 and B_SPAN >= NB_ROWS + B_GROUP - 1 and 2 * GRID_W == LANES
    b = rel_bias.astype(F32)
    pad = GRID_W - NB_COLS
    bpad = jnp.concatenate([jnp.repeat(b[..., :1], pad, axis=-1), b,
                            jnp.repeat(b[..., -1:], pad + 1, axis=-1)], axis=-1)
    off = pad + NB_COLS - 1
    rows = jnp.stack([jnp.roll(bpad, -off, axis=-1), jnp.roll(bpad, GRID_W - off, axis=-1)], axis=-3)
    return jnp.pad(rows, [(0, 0)] * (rows.ndim - 2) + [(0, 1), (0, 0)])


def _fill_bias_table(tab_ref, rows_ref, kind):
    if kind == B_TABLE_CTX:
        tab_ref[...] = jnp.full(tab_ref.shape, -jnp.inf, F32)
        return
    first_row, dr0 = B_LAYOUTS[kind]
    lane = lax.broadcasted_iota(jnp.int32, (GRID_W, LANES), 1)
    qc = lax.broadcasted_iota(jnp.int32, (GRID_W, LANES), 0)
    left_lanes = lane < GRID_W
    kc = lane & (GRID_W - 1)
    cstart = jnp.clip(qc - NB_COLS // 2, 0, GRID_W - NB_COLS)
    col_ok = (kc >= cstart) & (kc < cstart + NB_COLS)

    def per_head(h, carry):
        pairs = {}

        def pair(p):
            if p not in pairs:
                halves = [pltpu.roll(jnp.broadcast_to(rows_ref[h, c, dr:dr + 1, :], (GRID_W, LANES)),
                                     0, 1, stride=1, stride_axis=0)
                          for c, dr in ((0, max(p - 1, 0)), (1, min(p, N_DR - 1)))]
                pairs[p] = jnp.where(col_ok, jnp.where(left_lanes, halves[0], halves[1]), -jnp.inf)
            return pairs[p]

        for j in range(B_GROUP):
            for p in range(B_SPAN // 2):
                lj = 2 * p
                vis = [first_row(j) <= r < first_row(j) + NB_ROWS for r in (lj, lj + 1)]
                block = pair(lj - j + dr0 + 1) if any(vis) else None
                if vis == [True, False]:
                    block = jnp.where(left_lanes, block, -jnp.inf)
                elif vis == [False, True]:
                    block = jnp.where(left_lanes, -jnp.inf, block)
                elif block is None:
                    block = jnp.full((GRID_W, LANES), -jnp.inf, F32)
                tab_ref[h, j * GRID_W:(j + 1) * GRID_W, p * LANES:(p + 1) * LANES] = block
        return carry

    lax.fori_loop(0, B_HEADS, per_head, 0)


def _attn_b_kernel(q_ref, k_ref, v_ref, rows_ref, o_ref, tab_ref, *, n_lat_rows, n_ctx_rows):
    g = pl.program_id(0)
    n_lat_groups = n_lat_rows // ATTN_ROWS
    for step, kind in ((0, B_TABLE_FIRST), (1, B_TABLE_INNER), (n_lat_groups - 1, B_TABLE_LAST),
                       (n_lat_groups, B_TABLE_CTX)):
        pl.when(g == step)(functools.partial(_fill_bias_table, tab_ref, rows_ref, kind))
    grid_rows = n_lat_rows // GRID_W
    win = B_SPAN * GRID_W
    first = jnp.clip(g * B_GROUP - NB_ROWS // 2, 0, grid_rows - B_SPAN)
    start = pl.multiple_of(first * GRID_W, GRID_W)
    kb = k_ref[pl.ds(start, win), :]
    vb = v_ref[pl.ds(start, win), :]
    kc = k_ref[pl.ds(n_lat_rows, n_ctx_rows), :]
    vc = v_ref[pl.ds(n_lat_rows, n_ctx_rows), :]
    q = q_ref[...] * (B_HEAD_DIM ** -0.5)
    outs = []
    for h in range(B_HEADS):
        hs = slice(h * B_HEAD_DIM, (h + 1) * B_HEAD_DIM)
        outs.append(_attend(q[:, hs], kb[:, hs], vb[:, hs], kc[:, hs], vc[:, hs], tab_ref[h]))
    o_ref[...] = jnp.concatenate(outs, axis=1).astype(o_ref.dtype)


def _attn_b(zb, bias_rows, layer, n_lat_rows):
    t = zb.shape[0]
    w = BRANCH_W
    win = B_SPAN * GRID_W
    assert n_lat_rows // GRID_W >= B_SPAN and n_lat_rows // ATTN_ROWS >= 2
    resident = pl.Buffered(1)
    rows_block = (None, B_HEADS, 2, N_DR + 1, LANES)
    resident_bytes = (2 * _nbytes((t, w), BF16) + _nbytes(rows_block[1:], F32)
                      + _nbytes((B_HEADS, ATTN_ROWS, win), F32))
    return pl.pallas_call(
        functools.partial(_attn_b_kernel, n_lat_rows=n_lat_rows, n_ctx_rows=t - n_lat_rows),
        out_shape=jax.ShapeDtypeStruct((t, w), BF16),
        grid=(t // ATTN_ROWS,),
        in_specs=[pl.BlockSpec((ATTN_ROWS, w), lambda g: (g, 0)),
                  pl.BlockSpec((t, w), lambda g: (0, 1), pipeline_mode=resident),
                  pl.BlockSpec((t, w), lambda g: (0, 2), pipeline_mode=resident),
                  pl.BlockSpec(rows_block, lambda g: (layer, 0, 0, 0, 0), pipeline_mode=resident)],
        out_specs=pl.BlockSpec((ATTN_ROWS, w), lambda g: (g, 0)),
        scratch_shapes=[pltpu.VMEM((B_HEADS, ATTN_ROWS, win), F32)],
        compiler_params=_params(("arbitrary",), [_nbytes((ATTN_ROWS, w), BF16)] * 2,
                                resident_bytes + (16 << 20)),
        name="attn_b",
    )(zb, zb, zb, bias_rows)


def _cd_prep_kernel(xr_ref, xr_p, xr_n, xd_ref, xd_p, xd_n, cd_ref, cd_p, cd_n, bd_ref,
                    cw_ref, cb_ref, dw_ref, u_ref, yd_ref, *, n_lat_chunks, n_chunks):
    c = pl.program_id(0)
    has_prev = (c != 0) & (c != n_lat_chunks)
    has_next = (c != n_lat_chunks - 1) & (c != n_chunks - 1)
    ch = xr_ref.shape[0]
    h = SUBLANES

    def halo_cat(cur, prev, nxt):
        return jnp.concatenate([jnp.where(has_prev, prev, 0.0), cur, jnp.where(has_next, nxt, 0.0)], axis=0)

    xr = halo_cat(xr_ref[...], xr_p[...], xr_n[...])
    left = C_CONV // 2
    u = xr[h - left:h - left + ch] * cw_ref[0:1, :]
    for j in range(1, C_CONV):
        u = u + xr[h - left + j:h - left + j + ch] * cw_ref[j:j + 1, :]
    u_ref[...] = u + cb_ref[...]

    pd = halo_cat(cd_ref[...] * xd_ref[...], cd_p[...] * xd_p[...], cd_n[...] * xd_n[...])
    left = D_CONV // 2
    y = pd[h - left:h - left + ch] * dw_ref[0:1, :]
    for j in range(1, D_CONV):
        y = y + pd[h - left + j:h - left + j + ch] * dw_ref[j:j + 1, :]
    yd_ref[...] = (bd_ref[...] * y).astype(yd_ref.dtype)


def _cd_prep(zcd, conv_w, conv_b, d_conv_w, n_lat_rows):
    t = zcd.shape[0]
    w = BRANCH_W
    ch = ROW_CHUNK
    n_chunks = t // ch
    per = ch // SUBLANES
    last = t // SUBLANES - 1

    def cur(col):
        return pl.BlockSpec((ch, w), lambda c: (c, col))

    def prev(col):
        return pl.BlockSpec((SUBLANES, w), lambda c: (jnp.maximum(c * per - 1, 0), col))

    def nxt(col):
        return pl.BlockSpec((SUBLANES, w), lambda c: (jnp.minimum((c + 1) * per, last), col))

    small = lambda rows: pl.BlockSpec((rows, w), lambda c: (0, 0))
    return pl.pallas_call(
        functools.partial(_cd_prep_kernel, n_lat_chunks=n_lat_rows // ch, n_chunks=n_chunks),
        out_shape=(jax.ShapeDtypeStruct((t, w), F32), jax.ShapeDtypeStruct((t, w), BF16)),
        grid=(n_chunks,),
        in_specs=[cur(0), prev(0), nxt(0), cur(2), prev(2), nxt(2), cur(4), prev(4), nxt(4), cur(3),
                  small(C_CONV), small(1), small(D_CONV)],
        out_specs=(pl.BlockSpec((ch, w), lambda c: (c, 0)), pl.BlockSpec((ch, w), lambda c: (c, 0))),
        compiler_params=_params(("arbitrary",), [_nbytes((ch, w), F32)] * 6, 8 << 20),
        name="cd_prep",
    )(zcd, zcd, zcd, zcd, zcd, zcd, zcd, zcd, zcd, zcd, conv_w, conv_b.reshape(1, w), d_conv_w)


def _sigmoid(x):
    return 0.5 * jnp.tanh(0.5 * x) + 0.5


def _tile_scan(a, b, reverse):
    rows = lax.broadcasted_iota(jnp.int32, a.shape, 0)
    for k in (1, 2, 4):
        if reverse:
            a_sh, b_sh, valid = pltpu.roll(a, SUBLANES - k, 0), pltpu.roll(b, SUBLANES - k, 0), rows < SUBLANES - k
        else:
            a_sh, b_sh, valid = pltpu.roll(a, k, 0), pltpu.roll(b, k, 0), rows >= k
        b = jnp.where(valid, a * b_sh + b, b)
        a = jnp.where(valid, a * a_sh, a)
    return a, b


def _scan_kernel(uf_ref, ub_ref, wg_ref, ba_ref, bx_ref, lam_ref, hf_ref, hb_ref,
                 a_scr, b_scr, carry_f, carry_b):
    @pl.when(pl.program_id(0) == 0)
    def _():
        carry_f[...] = jnp.zeros_like(carry_f)
        carry_b[...] = jnp.zeros_like(carry_b)

    ch, w = uf_ref.shape
    n_tiles = ch // SUBLANES

    def gates(u_ref, d):
        u = u_ref[...]
        ub = u.astype(BF16)
        sp = jax.nn.softplus(-lam_ref[d:d + 1, :])
        for blk in range(C_BLOCKS):
            sl = slice(blk * C_BLOCK_W, (blk + 1) * C_BLOCK_W)
            g = jnp.dot(ub[:, sl], wg_ref[d, blk], preferred_element_type=F32)
            r = _sigmoid(g[:, :C_BLOCK_W] + ba_ref[d:d + 1, sl])
            i = _sigmoid(g[:, C_BLOCK_W:] + bx_ref[d:d + 1, sl])
            log_a = -C_POW * r * sp[:, sl]
            a = jnp.exp(log_a)
            a_scr[:, sl] = a
            b_scr[:, sl] = jnp.sqrt(-jnp.tanh(log_a) * (a * a + 1.0)) * (i * u[:, sl])

    def scan(out_ref, carry_ref, reverse):
        def body(step, carry):
            tile = n_tiles - 1 - step if reverse else step
            r0 = pl.multiple_of(tile * SUBLANES, SUBLANES)
            a, b = _tile_scan(a_scr[pl.ds(r0, SUBLANES), :], b_scr[pl.ds(r0, SUBLANES), :], reverse)
            h = a * carry + b
            out_ref[pl.ds(r0, SUBLANES), :] = h
            edge = h[0:1, :] if reverse else h[SUBLANES - 1:SUBLANES, :]
            return jnp.broadcast_to(edge, h.shape)

        carry_ref[...] = lax.fori_loop(0, n_tiles, body, carry_ref[...], unroll=4)

    gates(uf_ref, 0)
    scan(hf_ref, carry_f, False)
    gates(ub_ref, 1)
    scan(hb_ref, carry_b, True)


def _rglru_scan(u, wg, b_a, b_x, lam, n_lat_rows):
    t, w = u.shape
    ch = ROW_CHUNK
    n_chunks = t // ch
    n_lat = n_lat_rows // ch
    n_ctx = n_chunks - n_lat

    def fwd_chunk(i):
        return jnp.where(i < n_ctx, n_lat + i, i - n_ctx)

    def bwd_chunk(i):
        return n_chunks - 1 - i

    vec = pl.BlockSpec((2, w), lambda i: (0, 0))
    return pl.pallas_call(
        _scan_kernel,
        out_shape=(jax.ShapeDtypeStruct((t, w), F32), jax.ShapeDtypeStruct((t, w), F32)),
        grid=(n_chunks,),
        in_specs=[pl.BlockSpec((ch, w), lambda i: (fwd_chunk(i), 0)),
                  pl.BlockSpec((ch, w), lambda i: (bwd_chunk(i), 0)),
                  pl.BlockSpec((2, C_BLOCKS, C_BLOCK_W, 2 * C_BLOCK_W), lambda i: (0, 0, 0, 0)),
                  vec, vec, vec],
        out_specs=(pl.BlockSpec((ch, w), lambda i: (fwd_chunk(i), 0)),
                   pl.BlockSpec((ch, w), lambda i: (bwd_chunk(i), 0))),
        scratch_shapes=[pltpu.VMEM((ch, w), F32), pltpu.VMEM((ch, w), F32),
                        pltpu.VMEM((SUBLANES, w), F32), pltpu.VMEM((SUBLANES, w), F32)],
        compiler_params=_params(("arbitrary",), [_nbytes((ch, w), F32)] * 4, 8 << 20),
        name="rglru_scan",
    )(u, u, wg, b_a, b_x, lam)


def _c_final_kernel(hf_ref, hb_ref, gr_ref, o_ref):
    o_ref[...] = ((hf_ref[...] + hb_ref[...]) * jax.nn.gelu(gr_ref[...])).astype(o_ref.dtype)


def _c_final(hf, hb, zcd):
    t, w = hf.shape
    rows = _pick_tile(t, 1024, SUBLANES)
    spec = pl.BlockSpec((rows, w), lambda i: (i, 0))
    return pl.pallas_call(
        _c_final_kernel,
        out_shape=jax.ShapeDtypeStruct((t, w), BF16),
        grid=(t // rows,),
        in_specs=[spec, spec, pl.BlockSpec((rows, w), lambda i: (i, 1))],
        out_specs=spec,
        compiler_params=_params(("arbitrary",), [_nbytes((rows, w), F32)] * 4, 4 << 20),
        name="c_final",
    )(hf, hb, zcd)


def _merge_kernel(h_ref, ya_ref, yb_ref, yr_ref, yd_ref, wg0, wg1, wg2, wg3, wb_ref, bg_ref, o_ref,
                  wg_scr, wb_scr):
    @pl.when(pl.program_id(1) == 0)
    def _():
        for k, wg_ref in enumerate((wg0, wg1, wg2, wg3)):
            wg_scr[k] = wg_ref[...].astype(BF16)
        wb_scr[...] = wb_ref[...].astype(BF16)

    h = h_ref[...]
    merged = None
    for k, y_ref in enumerate((ya_ref, yb_ref, yr_ref, yd_ref)):
        g = jax.nn.sigmoid(jnp.dot(h, wg_scr[k], preferred_element_type=F32) + bg_ref[k:k + 1, :])
        term = g * jnp.dot(y_ref[...], wb_scr[k], preferred_element_type=F32)
        merged = term if merged is None else merged + term
    o_ref[...] = merged.astype(o_ref.dtype)


def _merge(h, ys, w_in, w_branch, b_gate, layer, *, tm, tn):
    t, d = h.shape
    per = d // tn
    assert GATE_COL0 % tn == 0
    gate_block0 = GATE_COL0 // tn
    y_spec = pl.BlockSpec((tm, BRANCH_W), lambda j, i: (i, 0))

    def gate_spec(k):
        return pl.BlockSpec((None, d, tn), lambda j, i: (layer, 0, gate_block0 + k * per + j))

    blocks = ([_nbytes((tm, d), BF16)] + [_nbytes((tm, BRANCH_W), BF16)] * N_BRANCH
              + [_nbytes((d, tn), F32)] * N_BRANCH + [_nbytes((N_BRANCH, BRANCH_W, tn), F32)]
              + [_nbytes((tm, tn), BF16)])
    scratch_bytes = _nbytes((N_BRANCH, d, tn), BF16) + _nbytes((N_BRANCH, BRANCH_W, tn), BF16)
    return pl.pallas_call(
        _merge_kernel,
        out_shape=jax.ShapeDtypeStruct((t, d), BF16),
        grid=(per, t // tm),
        in_specs=[pl.BlockSpec((tm, d), lambda j, i: (i, 0)), y_spec, y_spec, y_spec, y_spec,
                  gate_spec(0), gate_spec(1), gate_spec(2), gate_spec(3),
                  pl.BlockSpec((None, N_BRANCH, BRANCH_W, tn), lambda j, i: (layer, 0, 0, j)),
                  pl.BlockSpec((None, N_BRANCH, tn), lambda j, i: (layer, 0, j))],
        out_specs=pl.BlockSpec((tm, tn), lambda j, i: (i, j)),
        scratch_shapes=[pltpu.VMEM((N_BRANCH, d, tn), BF16), pltpu.VMEM((N_BRANCH, BRANCH_W, tn), BF16)],
        compiler_params=_params(("arbitrary", "arbitrary"), blocks,
                                scratch_bytes + 6 * _nbytes((tm, tn), F32)),
        name="merge",
    )(h, *ys, w_in, w_in, w_in, w_in, w_branch, b_gate)


def _ffn_in_kernel(h_ref, wg_ref, wu_ref, wo_ref, o_ref, wo_bf_ref, *, n_cast_steps, per):
    h = h_ref[...]
    g = jnp.dot(h, wg_ref[...].astype(BF16), preferred_element_type=F32)
    u = jnp.dot(h, wu_ref[...].astype(BF16), preferred_element_type=F32)
    o_ref[...] = (jax.nn.silu(g) * u).astype(o_ref.dtype)

    @pl.when(pl.program_id(0) * per + pl.program_id(1) < n_cast_steps)
    def _():
        wo_bf_ref[...] = wo_ref[...].astype(BF16)


def _ffn_in(h, w_ffn_in, w_ffn_out, layer, *, tm, tn):
    t, d = h.shape
    per = FFN_HIDDEN // tn
    n_steps = (t // tm) * per
    cast_rows = min(r for r in range(BF16_ROWS, FFN_HIDDEN + 1, BF16_ROWS)
                    if FFN_HIDDEN % r == 0 and FFN_HIDDEN // r <= n_steps)
    n_cast_steps = FFN_HIDDEN // cast_rows

    def cast_block(i, j):
        return (jnp.minimum(i * per + j, n_cast_steps - 1), 0)

    blocks = [_nbytes((tm, d), BF16), _nbytes((d, tn), F32), _nbytes((d, tn), F32), _nbytes((tm, tn), BF16),
              _nbytes((cast_rows, d), F32), _nbytes((cast_rows, d), BF16)]
    return pl.pallas_call(
        functools.partial(_ffn_in_kernel, n_cast_steps=n_cast_steps, per=per),
        out_shape=(jax.ShapeDtypeStruct((t, FFN_HIDDEN), BF16),
                   jax.ShapeDtypeStruct((FFN_HIDDEN, d), BF16)),
        grid=(t // tm, per),
        in_specs=[pl.BlockSpec((tm, d), lambda i, j: (i, 0)),
                  pl.BlockSpec((None, d, tn), lambda i, j: (layer, 0, j)),
                  pl.BlockSpec((None, d, tn), lambda i, j: (layer, 0, per + j)),
                  pl.BlockSpec((None, cast_rows, d), lambda i, j: (layer,) + cast_block(i, j))],
        out_specs=(pl.BlockSpec((tm, tn), lambda i, j: (i, j)),
                   pl.BlockSpec((cast_rows, d), cast_block)),
        compiler_params=_params(("arbitrary", "arbitrary"), blocks, 4 * _nbytes((tm, tn), F32)),
        name="ffn_in",
    )(h, w_ffn_in, w_ffn_in, w_ffn_out)


def _rope_tables(n_lat_rows, n_ctx_rows):
    half = A_HEAD_DIM // 4
    t = jnp.arange(n_lat_rows)
    freqs = ROPE_THETA ** (-jnp.arange(half, dtype=F32) / half)

    def cos_sin(pos):
        ang = pos.astype(F32)[:, None] * freqs[None, :]
        return jnp.cos(ang), jnp.sin(ang)

    cr, sr = cos_sin(t // GRID_W)
    cc, sc = cos_sin(t % GRID_W)
    cos = jnp.concatenate([cr, cr, cc, cc], axis=-1)
    sin = jnp.concatenate([-sr, sr, -sc, sc], axis=-1)
    cos = jnp.concatenate([cos, jnp.ones((n_ctx_rows, A_HEAD_DIM), F32)], axis=0)
    sin = jnp.concatenate([sin, jnp.zeros((n_ctx_rows, A_HEAD_DIM), F32)], axis=0)
    reps = LANES // A_HEAD_DIM
    return jnp.tile(cos, (1, reps)), jnp.tile(sin, (1, reps))


def kernel(x, c, ctx, c_ctx, w_mod, b_mod, norm1, norm2, w_in, b_gate, a_sink, nb_bias, c_conv_w, c_conv_b,
           c_w_a, c_b_a, c_w_x, c_b_x, c_lam, d_conv_w, w_branch, w_out, w_ffn_in, w_ffn_out, final_norm):
    bsz, s, d = x.shape
    l_ctx = ctx.shape[1]
    assert bsz == 1 and d == D_MODEL and c.shape[0] == 1
    assert s % ROW_CHUNK == 0 and l_ctx % ROW_CHUNK == 0 and s // GRID_W >= NB_ROWS and s >= 3 * A_BLOCK
    t = s + l_ctx
    depth = w_mod.shape[0]

    wg_scan = jnp.concatenate([c_w_a, c_w_x], axis=-1).astype(BF16)

    cos_t, sin_t = _rope_tables(s, l_ctx)
    bias_rows = _nb_bias_rows(nb_bias)

    cs = jnp.concatenate([c, c_ctx[None, :], jnp.zeros((SUBLANES - 2, d), F32)], axis=0)
    mod_all = _modulation(cs, w_mod, b_mod)
    mod_all = mod_all[:, :2].reshape(depth, 2, N_MOD, d).transpose(0, 2, 1, 3)

    xt = jnp.concatenate([x[0], ctx[0]], axis=0)

    tm_big = _pick_tile(t, 1408)
    tm_huge = _pick_tile(t, 2816)
    tm_small = _pick_tile(t, 704)
    h = _norm_mod(xt, norm1[0], mod_all[0], 0, s)
    for l in range(depth):
        mod = mod_all[l]
        za = _matmul(h, w_in, l, col0=0, n=A_COLS, tm=tm_big, tn=A_COLS, out_dtype=BF16, name="proj_a",
                     rope=(cos_t, sin_t))
        zb = _matmul(h, w_in, l, col0=A_COLS, n=B_COLS, tm=tm_big, tn=A_COLS, out_dtype=BF16, name="proj_b")
        zcd = _matmul(h, w_in, l, col0=A_COLS + B_COLS, n=CD_COLS, tm=tm_huge, tn=256, out_dtype=F32,
                      name="proj_cd")
        ya = _attn_a(za, a_sink[l], s)
        yb = _attn_b(zb, bias_rows, l, s)
        u, yd = _cd_prep(zcd, c_conv_w[l], c_conv_b[l], d_conv_w[l], s)
        hf, hb = _rglru_scan(u, wg_scan[l], c_b_a[l], c_b_x[l], c_lam[l], s)
        yr = _c_final(hf, hb, zcd)
        merged = _merge(h, (ya, yb, yr, yd), w_in, w_branch, b_gate, l, tm=tm_small, tn=256)
        xt, h2 = _resid_norm(merged, w_out, l, xt, mod, 2, norm2[l], mod, 3, s, name="proj_out")
        act, w_ffn_out_b = _ffn_in(h2, w_ffn_in, w_ffn_out, l, tm=tm_big, tn=512)
        if l + 1 < depth:
            xt, h = _resid_norm(act, w_ffn_out_b, None, xt, mod, 5, norm1[l + 1], mod_all[l + 1], 0, s,
                                name="ffn_out")
    out = _resid_final(act, w_ffn_out_b, None, xt, mod_all[depth - 1], 5, final_norm, s, name="ffn_out_final")
    return out[None]
```

```python
import functools

import numpy as np
import jax
import jax.numpy as jnp
from jax import lax
from jax.experimental import pallas as pl
from jax.experimental.pallas import tpu as pltpu

F32 = jnp.float32
BF16 = jnp.bfloat16

D_MODEL = 2048
DEPTH = 4
GRID_W = 64
EPS = 1e-6
N_MOD = 6
N_BRANCH = 4
BRANCH_W = 512
A_HEADS, A_KV_HEADS, A_HEAD_DIM, A_WINDOW, A_BLOCK = 8, 2, 64, 128, 128
ROPE_THETA = 10000.0
B_HEADS, B_HEAD_DIM, NB_ROWS, NB_COLS = 8, 64, 8, 16
C_BLOCKS, C_BLOCK_W, C_CONV, C_POW = 4, 128, 4, 8.0
D_CONV = 3
FFN_HIDDEN = 5632
A_COLS = (A_HEADS + 2 * A_KV_HEADS) * A_HEAD_DIM
A_ROPE_COLS = (A_HEADS + A_KV_HEADS) * A_HEAD_DIM
B_COLS = 3 * BRANCH_W
CD_COLS = 5 * BRANCH_W
GATE_COL0 = A_COLS + B_COLS + CD_COLS

LANES = 128
SUBLANES = 8
BF16_ROWS = 16
VMEM_BYTES = 64 * 1024 * 1024
VMEM_CAP = VMEM_BYTES - 8 * 1024 * 1024

ROW_CHUNK = 256
ATTN_ROWS = 256


def _pick_tile(total, target, mult=BF16_ROWS):
    best = None
    for d in range(mult, min(total, target) + 1, mult):
        if total % d == 0:
            best = d
    assert best is not None, (total, target, mult)
    return best


def _params(semantics, block_bytes, extra_bytes=0):
    need = 2 * sum(block_bytes) + extra_bytes + (4 << 20)
    return pltpu.CompilerParams(dimension_semantics=semantics,
                                vmem_limit_bytes=int(min(max(need, 16 << 20), VMEM_CAP)))


def _nbytes(shape, dtype):
    return int(np.prod(shape)) * jnp.dtype(dtype).itemsize


def _mod_kernel(cs_ref, w_ref, b_ref, o_ref):
    s = jax.nn.silu(cs_ref[...]).astype(BF16)
    w = w_ref[0].astype(BF16)
    o_ref[0] = jnp.dot(s, w, preferred_element_type=F32) + b_ref[0]


def _modulation(cs, w_mod, b_mod):
    depth, d, n = w_mod.shape
    tn = 1024
    return pl.pallas_call(
        _mod_kernel,
        out_shape=jax.ShapeDtypeStruct((depth, SUBLANES, n), F32),
        grid=(depth, n // tn),
        in_specs=[pl.BlockSpec((SUBLANES, d), lambda l, j: (0, 0)),
                  pl.BlockSpec((1, d, tn), lambda l, j: (l, 0, j)),
                  pl.BlockSpec((1, 1, tn), lambda l, j: (l, 0, j))],
        out_specs=pl.BlockSpec((1, SUBLANES, tn), lambda l, j: (l, 0, j)),
        compiler_params=_params(("arbitrary", "arbitrary"), [_nbytes((d, tn), F32)],
                                _nbytes((d, tn), BF16)),
        name="modulation",
    )(cs, w_mod, b_mod.reshape(depth, 1, n))


def _norm_mod_kernel(xl_ref, xc_ref, g_ref, mod_ref, xo_ref, o_ref, *, shift_idx, n_lat_blocks):
    is_ctx = pl.program_id(0) >= n_lat_blocks
    x = jnp.where(is_ctx, xc_ref[...], xl_ref[...])
    xo_ref[...] = x
    y = x * lax.rsqrt(jnp.mean(x * x, axis=-1, keepdims=True) + EPS)
    y = y * g_ref[...]
    shift2 = mod_ref[shift_idx]
    scale2 = mod_ref[shift_idx + 1]
    shift = jnp.where(is_ctx, shift2[1:2], shift2[0:1])
    scale = jnp.where(is_ctx, scale2[1:2], scale2[0:1])
    o_ref[...] = (y * (1 + scale) + shift).astype(o_ref.dtype)


def _norm_mod(x_lat, x_ctx, g, mod, shift_idx):
    n_lat_rows, d = x_lat.shape
    t = n_lat_rows + x_ctx.shape[0]
    n_lat = n_lat_rows // ROW_CHUNK
    rows = pl.BlockSpec((ROW_CHUNK, d), lambda i: (i, 0))
    return pl.pallas_call(
        functools.partial(_norm_mod_kernel, shift_idx=shift_idx, n_lat_blocks=n_lat),
        out_shape=(jax.ShapeDtypeStruct((t, d), F32), jax.ShapeDtypeStruct((t, d), BF16)),
        grid=(t // ROW_CHUNK,),
        in_specs=[pl.BlockSpec((ROW_CHUNK, d), lambda i: (jnp.minimum(i, n_lat - 1), 0)),
                  pl.BlockSpec((ROW_CHUNK, d), lambda i: (jnp.maximum(i - n_lat, 0), 0)),
                  pl.BlockSpec((1, d), lambda i: (0, 0)),
                  pl.BlockSpec((N_MOD, 2, d), lambda i: (0, 0, 0))],
        out_specs=(rows, rows),
        compiler_params=_params(("arbitrary",), [_nbytes((ROW_CHUNK, d), F32)] * 4,
                                4 * _nbytes((ROW_CHUNK, d), F32)),
        name="norm_mod",
    )(x_lat, x_ctx, g.reshape(1, d), mod)


def _rope_rotate_half(x):
    lane = lax.broadcasted_iota(jnp.int32, x.shape, 1)
    first = (lane & 16) == 0
    return jnp.where(first, pltpu.roll(x, LANES - 16, 1), pltpu.roll(x, 16, 1))


def _mm_cast_kernel(a_ref, w_ref, o_ref):
    w = w_ref[...].astype(BF16)
    o_ref[...] = jnp.dot(a_ref[...], w, preferred_element_type=F32).astype(o_ref.dtype)


def _mm_rope_kernel(a_ref, w_ref, cos_ref, sin_ref, o_ref):
    acc = jnp.dot(a_ref[...], w_ref[...].astype(BF16), preferred_element_type=F32)
    cos = cos_ref[...]
    sin = sin_ref[...]
    for c in range(A_ROPE_COLS // LANES):
        xc = acc[:, c * LANES:(c + 1) * LANES]
        o_ref[:, c * LANES:(c + 1) * LANES] = (xc * cos + _rope_rotate_half(xc) * sin).astype(o_ref.dtype)
    o_ref[:, A_ROPE_COLS:] = acc[:, A_ROPE_COLS:].astype(o_ref.dtype)


def _matmul(a, w, layer, *, col0, n, tm, tn, out_dtype, name, rope=None):
    t, k = a.shape
    assert col0 % tn == 0 and n % tn == 0 and w.dtype == F32
    cb0 = col0 // tn
    grid = (t // tm, n // tn)
    in_specs = [pl.BlockSpec((tm, k), lambda i, j: (i, 0)),
                pl.BlockSpec((None, k, tn), lambda i, j: (layer, 0, cb0 + j))]
    operands = [a, w]
    blocks = [_nbytes((tm, k), BF16), _nbytes((k, tn), F32), _nbytes((tm, tn), out_dtype)]
    if rope is not None:
        kernel = _mm_rope_kernel
        in_specs += [pl.BlockSpec((tm, LANES), lambda i, j: (i, 0))] * 2
        operands += list(rope)
    else:
        kernel = _mm_cast_kernel
    return pl.pallas_call(
        kernel,
        out_shape=jax.ShapeDtypeStruct((t, n), out_dtype),
        grid=grid,
        in_specs=in_specs,
        out_specs=pl.BlockSpec((tm, tn), lambda i, j: (i, j)),
        compiler_params=_params(("arbitrary", "arbitrary"), blocks,
                                3 * _nbytes((tm, tn), F32) + _nbytes((k, tn), BF16)),
        name=name,
    )(*operands)


CAST_ROWS = 128


def _pick_row(two_rows, is_ctx):
    return jnp.where(is_ctx, two_rows[1:2], two_rows[0:1])


def _rms(x, g):
    y = x * lax.rsqrt(jnp.mean(x * x, axis=-1, keepdims=True) + EPS)
    return y * g


def _resident_bf16(w_ref, scratch):
    if not scratch:
        return w_ref
    wb_ref, = scratch

    @pl.when(pl.program_id(0) == 0)
    def _():
        def body(c, carry):
            r = pl.multiple_of(c * CAST_ROWS, CAST_ROWS)
            wb_ref[pl.ds(r, CAST_ROWS), :] = w_ref[pl.ds(r, CAST_ROWS), :].astype(BF16)
            return carry
        lax.fori_loop(0, w_ref.shape[0] // CAST_ROWS, body, 0)

    return wb_ref


def _resid_norm_kernel(a_ref, w_ref, x_ref, gmod_ref, nmod_ref, g_ref, xo_ref, ho_ref, *scratch,
                       gate_idx, shift_idx, n_lat_blocks):
    w = _resident_bf16(w_ref, scratch)
    is_ctx = pl.program_id(0) >= n_lat_blocks
    gate = _pick_row(gmod_ref[gate_idx], is_ctx)
    scale1 = 1 + _pick_row(nmod_ref[shift_idx + 1], is_ctx)
    shift = _pick_row(nmod_ref[shift_idx], is_ctx)
    acc = jnp.dot(a_ref[...], w[...], preferred_element_type=F32)
    x = x_ref[...] + gate * acc
    xo_ref[...] = x
    ho_ref[...] = (_rms(x, g_ref[...]) * scale1 + shift).astype(ho_ref.dtype)


def _resid_final_kernel(a_ref, w_ref, x_ref, gmod_ref, g_ref, o_ref, *, gate_idx):
    acc = jnp.dot(a_ref[...], w_ref[...], preferred_element_type=F32)
    x = x_ref[...] + gmod_ref[gate_idx][0:1] * acc
    o_ref[...] = _rms(x, g_ref[...])


def _resident_spec(w, layer):
    k, n = w.shape[-2:]
    if w.ndim == 2:
        return pl.BlockSpec((k, n), lambda i: (0, 0), pipeline_mode=pl.Buffered(1))
    return pl.BlockSpec((None, k, n), lambda i: (layer, 0, 0), pipeline_mode=pl.Buffered(1))


def _resid_norm(a, w, layer, x, gmod, gate_idx, g, nmod, shift_idx, n_lat_rows, *, name):
    t, k = a.shape
    d = x.shape[1]
    tm = ROW_CHUNK
    cast_w = w.dtype != BF16
    rows = lambda width: pl.BlockSpec((tm, width), lambda i: (i, 0))
    mod_spec = pl.BlockSpec((N_MOD, 2, d), lambda i: (0, 0, 0))
    resident = _nbytes((k, d), w.dtype) + (_nbytes((k, d), BF16) if cast_w else 0)
    return pl.pallas_call(
        functools.partial(_resid_norm_kernel, gate_idx=gate_idx, shift_idx=shift_idx,
                          n_lat_blocks=n_lat_rows // tm),
        out_shape=(jax.ShapeDtypeStruct((t, d), F32), jax.ShapeDtypeStruct((t, d), BF16)),
        grid=(t // tm,),
        in_specs=[rows(k),
                  _resident_spec(w, layer),
                  rows(d), mod_spec, mod_spec, pl.BlockSpec((1, d), lambda i: (0, 0))],
        out_specs=(rows(d), rows(d)),
        scratch_shapes=[pltpu.VMEM((k, d), BF16)] if cast_w else [],
        compiler_params=_params(("arbitrary",),
                                [_nbytes((tm, k), BF16), _nbytes((tm, d), F32) * 2, _nbytes((tm, d), BF16)],
                                resident + 4 * _nbytes((tm, d), F32)),
        name=name,
    )(a, w, x, gmod, nmod, g.reshape(1, d))


def _resid_final(a, w, layer, x, gmod, gate_idx, g, n_rows, *, name):
    k = a.shape[1]
    d = x.shape[1]
    tm = ROW_CHUNK
    rows = lambda width: pl.BlockSpec((tm, width), lambda i: (i, 0))
    return pl.pallas_call(
        functools.partial(_resid_final_kernel, gate_idx=gate_idx),
        out_shape=jax.ShapeDtypeStruct((n_rows, d), F32),
        grid=(n_rows // tm,),
        in_specs=[rows(k),
                  _resident_spec(w, layer),
                  rows(d), pl.BlockSpec((N_MOD, 2, d), lambda i: (0, 0, 0)),
                  pl.BlockSpec((1, d), lambda i: (0, 0))],
        out_specs=rows(d),
        compiler_params=_params(("arbitrary",), [_nbytes((tm, k), BF16), _nbytes((tm, d), F32) * 2],
                                _nbytes((k, d), BF16) + 4 * _nbytes((tm, d), F32)),
        name=name,
    )(a, w, x, gmod, g.reshape(1, d))


def _dot_nt(a, b):
    return lax.dot_general(a, b, (((1,), (1,)), ((), ())), preferred_element_type=F32)


def _attend(q, kb, vb, kc, vc, bias, sink=None):
    s_loc = _dot_nt(q, kb) + bias
    s_ctx = _dot_nt(q, kc)
    m = jnp.maximum(jnp.max(s_loc, axis=-1, keepdims=True), jnp.max(s_ctx, axis=-1, keepdims=True))
    if sink is not None:
        m = jnp.maximum(m, sink)
    e_loc = jnp.exp(s_loc - m)
    e_ctx = jnp.exp(s_ctx - m)
    den = jnp.sum(e_loc, axis=-1, keepdims=True) + jnp.sum(e_ctx, axis=-1, keepdims=True)
    if sink is not None:
        den = den + jnp.exp(sink - m)
    o = (jnp.dot(e_loc.astype(BF16), vb, preferred_element_type=F32)
         + jnp.dot(e_ctx.astype(BF16), vc, preferred_element_type=F32))
    return o / den


def _attn_a_kernel(sink_ref, q_ref, k_ref, v_ref, o_ref, *, n_lat_rows, n_ctx_rows):
    n = pl.program_id(0)
    rows = q_ref.shape[0]
    band = rows + 2 * A_WINDOW
    is_lat = n * rows < n_lat_rows
    start = pl.multiple_of(jnp.clip(n * rows - A_WINDOW, 0, n_lat_rows - band), A_BLOCK)
    keys = jnp.concatenate([k_ref[pl.ds(start, band), :], k_ref[pl.ds(n_lat_rows, n_ctx_rows), :]], axis=0)
    vals = jnp.concatenate([v_ref[pl.ds(start, band), :], v_ref[pl.ds(n_lat_rows, n_ctx_rows), :]], axis=0)
    n_keys = band + n_ctx_rows
    qpos = n * rows + lax.broadcasted_iota(jnp.int32, (rows, n_keys), 0)
    col = lax.broadcasted_iota(jnp.int32, (rows, n_keys), 1)
    visible = (col >= band) | ((jnp.abs(qpos - (start + col)) <= A_WINDOW) & is_lat)
    bias = jnp.where(visible, 0.0, -jnp.inf)
    scale = A_HEAD_DIM ** -0.5
    grp = A_HEADS // A_KV_HEADS
    q = q_ref[...] * scale
    outs = []
    for kvh in range(A_KV_HEADS):
        kv = slice(kvh * A_HEAD_DIM, (kvh + 1) * A_HEAD_DIM)
        heads = range(kvh * grp, (kvh + 1) * grp)
        qs = jnp.concatenate([q[:, h * A_HEAD_DIM:(h + 1) * A_HEAD_DIM] for h in heads], axis=0)
        s_all = _dot_nt(qs, keys[:, kv])
        es, dens = [], []
        for gi, h in enumerate(heads):
            s = s_all[gi * rows:(gi + 1) * rows] + bias
            m = jnp.maximum(jnp.max(s, axis=-1, keepdims=True), sink_ref[h])
            e = jnp.exp(s - m)
            dens.append(jnp.sum(e, axis=-1, keepdims=True) + jnp.exp(sink_ref[h] - m))
            es.append(e.astype(BF16))
        o_all = jnp.dot(jnp.concatenate(es, axis=0), vals[:, kv], preferred_element_type=F32)
        outs += [o_all[gi * rows:(gi + 1) * rows] / dens[gi] for gi in range(grp)]
    o_ref[...] = jnp.concatenate(outs, axis=1).astype(o_ref.dtype)


def _attn_a(za, sink, n_lat_rows):
    t = za.shape[0]
    qw = A_HEADS * A_HEAD_DIM
    kvw = A_KV_HEADS * A_HEAD_DIM
    assert A_WINDOW == A_BLOCK and n_lat_rows >= ATTN_ROWS + 2 * A_WINDOW
    resident = pl.Buffered(1)
    return pl.pallas_call(
        functools.partial(_attn_a_kernel, n_lat_rows=n_lat_rows, n_ctx_rows=t - n_lat_rows),
        out_shape=jax.ShapeDtypeStruct((t, qw), BF16),
        grid=(t // ATTN_ROWS,),
        in_specs=[pl.BlockSpec(memory_space=pltpu.SMEM),
                  pl.BlockSpec((ATTN_ROWS, qw), lambda n: (n, 0)),
                  pl.BlockSpec((t, kvw), lambda n: (0, qw // kvw), pipeline_mode=resident),
                  pl.BlockSpec((t, kvw), lambda n: (0, qw // kvw + 1), pipeline_mode=resident)],
        out_specs=pl.BlockSpec((ATTN_ROWS, qw), lambda n: (n, 0)),
        compiler_params=_params(("arbitrary",), [_nbytes((t, kvw), BF16)] * 2, 16 << 20),
        name="attn_a",
    )(sink, za, za, za)


B_GROUP = ATTN_ROWS // GRID_W
B_SPAN = 12
B_TABLE_FIRST, B_TABLE_INNER, B_TABLE_LAST, B_TABLE_CTX = range(4)


N_DR = 2 * NB_ROWS - 1
B_LAYOUTS = {B_TABLE_FIRST: (lambda j: 0, NB_ROWS - 1),
             B_TABLE_INNER: (lambda j: j, NB_ROWS - 1 - NB_ROWS // 2),
             B_TABLE_LAST: (lambda j: B_SPAN - NB_ROWS, NB_ROWS - 1 + B_GROUP - B_SPAN)}


def _nb_bias_rows(rel_bias):
    assert B_GROUP == NB_ROWS // 2 and B_SPAN >= NB_ROWS + B_GROUP - 1 and 2 * GRID_W == LANES
    b = rel_bias.astype(F32)
    pad = GRID_W - NB_COLS
    bpad = jnp.concatenate([jnp.repeat(b[..., :1], pad, axis=-1), b,
                            jnp.repeat(b[..., -1:], pad + 1, axis=-1)], axis=-1)
    off = pad + NB_COLS - 1
    rows = jnp.stack([jnp.roll(bpad, -off, axis=-1), jnp.roll(bpad, GRID_W - off, axis=-1)], axis=-3)
    return jnp.pad(rows, [(0, 0)] * (rows.ndim - 2) + [(0, 1), (0, 0)])


def _fill_bias_table(tab_ref, rows_ref, kind):
    if kind == B_TABLE_CTX:
        tab_ref[...] = jnp.full(tab_ref.shape, -jnp.inf, F32)
        return
    first_row, dr0 = B_LAYOUTS[kind]
    lane = lax.broadcasted_iota(jnp.int32, (GRID_W, LANES), 1)
    qc = lax.broadcasted_iota(jnp.int32, (GRID_W, LANES), 0)
    left_lanes = lane < GRID_W
    kc = lane & (GRID_W - 1)
    cstart = jnp.clip(qc - NB_COLS // 2, 0, GRID_W - NB_COLS)
    col_ok = (kc >= cstart) & (kc < cstart + NB_COLS)

    def per_head(h, carry):
        pairs = {}

        def pair(p):
            if p not in pairs:
                halves = [pltpu.roll(jnp.broadcast_to(rows_ref[h, c, dr:dr + 1, :], (GRID_W, LANES)),
                                     0, 1, stride=1, stride_axis=0)
                          for c, dr in ((0, max(p - 1, 0)), (1, min(p, N_DR - 1)))]
                pairs[p] = jnp.where(col_ok, jnp.where(left_lanes, halves[0], halves[1]), -jnp.inf)
            return pairs[p]

        for j in range(B_GROUP):
            for p in range(B_SPAN // 2):
                lj = 2 * p
                vis = [first_row(j) <= r < first_row(j) + NB_ROWS for r in (lj, lj + 1)]
                block = pair(lj - j + dr0 + 1) if any(vis) else None
                if vis == [True, False]:
                    block = jnp.where(left_lanes, block, -jnp.inf)
                elif vis == [False, True]:
                    block = jnp.where(left_lanes, -jnp.inf, block)
                elif block is None:
                    block = jnp.full((GRID_W, LANES), -jnp.inf, F32)
                tab_ref[h, j * GRID_W:(j + 1) * GRID_W, p * LANES:(p + 1) * LANES] = block
        return carry

    lax.fori_loop(0, B_HEADS, per_head, 0)


def _attn_b_kernel(q_ref, k_ref, v_ref, rows_ref, o_ref, tab_ref, *, n_lat_rows, n_ctx_rows):
    g = pl.program_id(0)
    n_lat_groups = n_lat_rows // ATTN_ROWS
    for step, kind in ((0, B_TABLE_FIRST), (1, B_TABLE_INNER), (n_lat_groups - 1, B_TABLE_LAST),
                       (n_lat_groups, B_TABLE_CTX)):
        pl.when(g == step)(functools.partial(_fill_bias_table, tab_ref, rows_ref, kind))
    grid_rows = n_lat_rows // GRID_W
    win = B_SPAN * GRID_W
    first = jnp.clip(g * B_GROUP - NB_ROWS // 2, 0, grid_rows - B_SPAN)
    start = pl.multiple_of(first * GRID_W, GRID_W)
    kb = k_ref[pl.ds(start, win), :]
    vb = v_ref[pl.ds(start, win), :]
    kc = k_ref[pl.ds(n_lat_rows, n_ctx_rows), :]
    vc = v_ref[pl.ds(n_lat_rows, n_ctx_rows), :]
    q = q_ref[...] * (B_HEAD_DIM ** -0.5)
    outs = []
    for h in range(B_HEADS):
        hs = slice(h * B_HEAD_DIM, (h + 1) * B_HEAD_DIM)
        outs.append(_attend(q[:, hs], kb[:, hs], vb[:, hs], kc[:, hs], vc[:, hs], tab_ref[h]))
    o_ref[...] = jnp.concatenate(outs, axis=1).astype(o_ref.dtype)


def _attn_b(zb, bias_rows, layer, n_lat_rows):
    t = zb.shape[0]
    w = BRANCH_W
    win = B_SPAN * GRID_W
    assert n_lat_rows // GRID_W >= B_SPAN and n_lat_rows // ATTN_ROWS >= 2
    resident = pl.Buffered(1)
    rows_block = (None, B_HEADS, 2, N_DR + 1, LANES)
    resident_bytes = (2 * _nbytes((t, w), BF16) + _nbytes(rows_block[1:], F32)
                      + _nbytes((B_HEADS, ATTN_ROWS, win), F32))
    return pl.pallas_call(
        functools.partial(_attn_b_kernel, n_lat_rows=n_lat_rows, n_ctx_rows=t - n_lat_rows),
        out_shape=jax.ShapeDtypeStruct((t, w), BF16),
        grid=(t // ATTN_ROWS,),
        in_specs=[pl.BlockSpec((ATTN_ROWS, w), lambda g: (g, 0)),
                  pl.BlockSpec((t, w), lambda g: (0, 1), pipeline_mode=resident),
                  pl.BlockSpec((t, w), lambda g: (0, 2), pipeline_mode=resident),
                  pl.BlockSpec(rows_block, lambda g: (layer, 0, 0, 0, 0), pipeline_mode=resident)],
        out_specs=pl.BlockSpec((ATTN_ROWS, w), lambda g: (g, 0)),
        scratch_shapes=[pltpu.VMEM((B_HEADS, ATTN_ROWS, win), F32)],
        compiler_params=_params(("arbitrary",), [_nbytes((ATTN_ROWS, w), BF16)] * 2,
                                resident_bytes + (16 << 20)),
        name="attn_b",
    )(zb, zb, zb, bias_rows)


def _cd_prep_kernel(xr_ref, xr_p, xr_n, xd_ref, xd_p, xd_n, cd_ref, cd_p, cd_n, bd_ref,
                    cw_ref, cb_ref, dw_ref, u_ref, yd_ref, *, n_lat_chunks, n_chunks):
    c = pl.program_id(0)
    has_prev = (c != 0) & (c != n_lat_chunks)
    has_next = (c != n_lat_chunks - 1) & (c != n_chunks - 1)
    ch = xr_ref.shape[0]
    h = SUBLANES

    def halo_cat(cur, prev, nxt):
        return jnp.concatenate([jnp.where(has_prev, prev, 0.0), cur, jnp.where(has_next, nxt, 0.0)], axis=0)

    xr = halo_cat(xr_ref[...], xr_p[...], xr_n[...])
    left = C_CONV // 2
    u = xr[h - left:h - left + ch] * cw_ref[0:1, :]
    for j in range(1, C_CONV):
        u = u + xr[h - left + j:h - left + j + ch] * cw_ref[j:j + 1, :]
    u_ref[...] = u + cb_ref[...]

    pd = halo_cat(cd_ref[...] * xd_ref[...], cd_p[...] * xd_p[...], cd_n[...] * xd_n[...])
    left = D_CONV // 2
    y = pd[h - left:h - left + ch] * dw_ref[0:1, :]
    for j in range(1, D_CONV):
        y = y + pd[h - left + j:h - left + j + ch] * dw_ref[j:j + 1, :]
    yd_ref[...] = (bd_ref[...] * y).astype(yd_ref.dtype)


def _cd_prep(zcd, conv_w, conv_b, d_conv_w, n_lat_rows):
    t = zcd.shape[0]
    w = BRANCH_W
    ch = ROW_CHUNK
    n_chunks = t // ch
    per = ch // SUBLANES
    last = t // SUBLANES - 1

    def cur(col):
        return pl.BlockSpec((ch, w), lambda c: (c, col))

    def prev(col):
        return pl.BlockSpec((SUBLANES, w), lambda c: (jnp.maximum(c * per - 1, 0), col))

    def nxt(col):
        return pl.BlockSpec((SUBLANES, w), lambda c: (jnp.minimum((c + 1) * per, last), col))

    small = lambda rows: pl.BlockSpec((rows, w), lambda c: (0, 0))
    return pl.pallas_call(
        functools.partial(_cd_prep_kernel, n_lat_chunks=n_lat_rows // ch, n_chunks=n_chunks),
        out_shape=(jax.ShapeDtypeStruct((t, w), F32), jax.ShapeDtypeStruct((t, w), BF16)),
        grid=(n_chunks,),
        in_specs=[cur(0), prev(0), nxt(0), cur(2), prev(2), nxt(2), cur(4), prev(4), nxt(4), cur(3),
                  small(C_CONV), small(1), small(D_CONV)],
        out_specs=(pl.BlockSpec((ch, w), lambda c: (c, 0)), pl.BlockSpec((ch, w), lambda c: (c, 0))),
        compiler_params=_params(("arbitrary",), [_nbytes((ch, w), F32)] * 6, 8 << 20),
        name="cd_prep",
    )(zcd, zcd, zcd, zcd, zcd, zcd, zcd, zcd, zcd, zcd, conv_w, conv_b.reshape(1, w), d_conv_w)


def _sigmoid(x):
    return 0.5 * jnp.tanh(0.5 * x) + 0.5


def _tile_scan(a, b, reverse):
    rows = lax.broadcasted_iota(jnp.int32, a.shape, 0)
    for k in (1, 2, 4):
        if reverse:
            a_sh, b_sh, valid = pltpu.roll(a, SUBLANES - k, 0), pltpu.roll(b, SUBLANES - k, 0), rows < SUBLANES - k
        else:
            a_sh, b_sh, valid = pltpu.roll(a, k, 0), pltpu.roll(b, k, 0), rows >= k
        b = jnp.where(valid, a * b_sh + b, b)
        a = jnp.where(valid, a * a_sh, a)
    return a, b


def _scan_kernel(uf_ref, ub_ref, wg_ref, ba_ref, bx_ref, lam_ref, hf_ref, hb_ref,
                 a_scr, b_scr, carry_f, carry_b):
    @pl.when(pl.program_id(0) == 0)
    def _():
        carry_f[...] = jnp.zeros_like(carry_f)
        carry_b[...] = jnp.zeros_like(carry_b)

    ch, w = uf_ref.shape
    n_tiles = ch // SUBLANES

    def gates(u_ref, d):
        u = u_ref[...]
        ub = u.astype(BF16)
        sp = jax.nn.softplus(-lam_ref[d:d + 1, :])
        for blk in range(C_BLOCKS):
            sl = slice(blk * C_BLOCK_W, (blk + 1) * C_BLOCK_W)
            g = jnp.dot(ub[:, sl], wg_ref[d, blk], preferred_element_type=F32)
            r = _sigmoid(g[:, :C_BLOCK_W] + ba_ref[d:d + 1, sl])
            i = _sigmoid(g[:, C_BLOCK_W:] + bx_ref[d:d + 1, sl])
            log_a = -C_POW * r * sp[:, sl]
            a = jnp.exp(log_a)
            a_scr[:, sl] = a
            b_scr[:, sl] = jnp.sqrt(-jnp.tanh(log_a) * (a * a + 1.0)) * (i * u[:, sl])

    def scan(out_ref, carry_ref, reverse):
        def body(step, carry):
            tile = n_tiles - 1 - step if reverse else step
            r0 = pl.multiple_of(tile * SUBLANES, SUBLANES)
            a, b = _tile_scan(a_scr[pl.ds(r0, SUBLANES), :], b_scr[pl.ds(r0, SUBLANES), :], reverse)
            h = a * carry + b
            out_ref[pl.ds(r0, SUBLANES), :] = h
            edge = h[0:1, :] if reverse else h[SUBLANES - 1:SUBLANES, :]
            return jnp.broadcast_to(edge, h.shape)

        carry_ref[...] = lax.fori_loop(0, n_tiles, body, carry_ref[...], unroll=4)

    gates(uf_ref, 0)
    scan(hf_ref, carry_f, False)
    gates(ub_ref, 1)
    scan(hb_ref, carry_b, True)


def _rglru_scan(u, wg, b_a, b_x, lam, n_lat_rows):
    t, w = u.shape
    ch = ROW_CHUNK
    n_chunks = t // ch
    n_lat = n_lat_rows // ch
    n_ctx = n_chunks - n_lat

    def fwd_chunk(i):
        return jnp.where(i < n_ctx, n_lat + i, i - n_ctx)

    def bwd_chunk(i):
        return n_chunks - 1 - i

    vec = pl.BlockSpec((2, w), lambda i: (0, 0))
    return pl.pallas_call(
        _scan_kernel,
        out_shape=(jax.ShapeDtypeStruct((t, w), F32), jax.ShapeDtypeStruct((t, w), F32)),
        grid=(n_chunks,),
        in_specs=[pl.BlockSpec((ch, w), lambda i: (fwd_chunk(i), 0)),
                  pl.BlockSpec((ch, w), lambda i: (bwd_chunk(i), 0)),
                  pl.BlockSpec((2, C_BLOCKS, C_BLOCK_W, 2 * C_BLOCK_W), lambda i: (0, 0, 0, 0)),
                  vec, vec, vec],
        out_specs=(pl.BlockSpec((ch, w), lambda i: (fwd_chunk(i), 0)),
                   pl.BlockSpec((ch, w), lambda i: (bwd_chunk(i), 0))),
        scratch_shapes=[pltpu.VMEM((ch, w), F32), pltpu.VMEM((ch, w), F32),
                        pltpu.VMEM((SUBLANES, w), F32), pltpu.VMEM((SUBLANES, w), F32)],
        compiler_params=_params(("arbitrary",), [_nbytes((ch, w), F32)] * 4, 8 << 20),
        name="rglru_scan",
    )(u, u, wg, b_a, b_x, lam)


def _c_final_kernel(hf_ref, hb_ref, gr_ref, o_ref):
    o_ref[...] = ((hf_ref[...] + hb_ref[...]) * jax.nn.gelu(gr_ref[...])).astype(o_ref.dtype)


def _c_final(hf, hb, zcd):
    t, w = hf.shape
    rows = _pick_tile(t, 1024, SUBLANES)
    spec = pl.BlockSpec((rows, w), lambda i: (i, 0))
    return pl.pallas_call(
        _c_final_kernel,
        out_shape=jax.ShapeDtypeStruct((t, w), BF16),
        grid=(t // rows,),
        in_specs=[spec, spec, pl.BlockSpec((rows, w), lambda i: (i, 1))],
        out_specs=spec,
        compiler_params=_params(("arbitrary",), [_nbytes((rows, w), F32)] * 4, 4 << 20),
        name="c_final",
    )(hf, hb, zcd)


def _merge_kernel(h_ref, ya_ref, yb_ref, yr_ref, yd_ref, wg0, wg1, wg2, wg3, wb_ref, bg_ref, o_ref,
                  wg_scr, wb_scr):
    @pl.when(pl.program_id(1) == 0)
    def _():
        for k, wg_ref in enumerate((wg0, wg1, wg2, wg3)):
            wg_scr[k] = wg_ref[...].astype(BF16)
        wb_scr[...] = wb_ref[...].astype(BF16)

    h = h_ref[...]
    merged = None
    for k, y_ref in enumerate((ya_ref, yb_ref, yr_ref, yd_ref)):
        g = jax.nn.sigmoid(jnp.dot(h, wg_scr[k], preferred_element_type=F32) + bg_ref[k:k + 1, :])
        term = g * jnp.dot(y_ref[...], wb_scr[k], preferred_element_type=F32)
        merged = term if merged is None else merged + term
    o_ref[...] = merged.astype(o_ref.dtype)


def _merge(h, ys, w_in, w_branch, b_gate, layer, *, tm, tn):
    t, d = h.shape
    per = d // tn
    assert GATE_COL0 % tn == 0
    gate_block0 = GATE_COL0 // tn
    y_spec = pl.BlockSpec((tm, BRANCH_W), lambda j, i: (i, 0))

    def gate_spec(k):
        return pl.BlockSpec((None, d, tn), lambda j, i: (layer, 0, gate_block0 + k * per + j))

    blocks = ([_nbytes((tm, d), BF16)] + [_nbytes((tm, BRANCH_W), BF16)] * N_BRANCH
              + [_nbytes((d, tn), F32)] * N_BRANCH + [_nbytes((N_BRANCH, BRANCH_W, tn), F32)]
              + [_nbytes((tm, tn), BF16)])
    scratch_bytes = _nbytes((N_BRANCH, d, tn), BF16) + _nbytes((N_BRANCH, BRANCH_W, tn), BF16)
    return pl.pallas_call(
        _merge_kernel,
        out_shape=jax.ShapeDtypeStruct((t, d), BF16),
        grid=(per, t // tm),
        in_specs=[pl.BlockSpec((tm, d), lambda j, i: (i, 0)), y_spec, y_spec, y_spec, y_spec,
                  gate_spec(0), gate_spec(1), gate_spec(2), gate_spec(3),
                  pl.BlockSpec((None, N_BRANCH, BRANCH_W, tn), lambda j, i: (layer, 0, 0, j)),
                  pl.BlockSpec((None, N_BRANCH, tn), lambda j, i: (layer, 0, j))],
        out_specs=pl.BlockSpec((tm, tn), lambda j, i: (i, j)),
        scratch_shapes=[pltpu.VMEM((N_BRANCH, d, tn), BF16), pltpu.VMEM((N_BRANCH, BRANCH_W, tn), BF16)],
        compiler_params=_params(("arbitrary", "arbitrary"), blocks,
                                scratch_bytes + 6 * _nbytes((tm, tn), F32)),
        name="merge",
    )(h, *ys, w_in, w_in, w_in, w_in, w_branch, b_gate)


def _ffn_in_kernel(h_ref, wg_ref, wu_ref, wo_ref, o_ref, wo_bf_ref, *, n_cast_steps, per):
    h = h_ref[...]
    g = jnp.dot(h, wg_ref[...].astype(BF16), preferred_element_type=F32)
    u = jnp.dot(h, wu_ref[...].astype(BF16), preferred_element_type=F32)
    o_ref[...] = (jax.nn.silu(g) * u).astype(o_ref.dtype)

    @pl.when(pl.program_id(0) * per + pl.program_id(1) < n_cast_steps)
    def _():
        wo_bf_ref[...] = wo_ref[...].astype(BF16)


def _ffn_in(h, w_ffn_in, w_ffn_out, layer, *, tm, tn):
    t, d = h.shape
    per = FFN_HIDDEN // tn
    n_steps = (t // tm) * per
    cast_rows = min(r for r in range(BF16_ROWS, FFN_HIDDEN + 1, BF16_ROWS)
                    if FFN_HIDDEN % r == 0 and FFN_HIDDEN // r <= n_steps)
    n_cast_steps = FFN_HIDDEN // cast_rows

    def cast_block(i, j):
        return (jnp.minimum(i * per + j, n_cast_steps - 1), 0)

    blocks = [_nbytes((tm, d), BF16), _nbytes((d, tn), F32), _nbytes((d, tn), F32), _nbytes((tm, tn), BF16),
              _nbytes((cast_rows, d), F32), _nbytes((cast_rows, d), BF16)]
    return pl.pallas_call(
        functools.partial(_ffn_in_kernel, n_cast_steps=n_cast_steps, per=per),
        out_shape=(jax.ShapeDtypeStruct((t, FFN_HIDDEN), BF16),
                   jax.ShapeDtypeStruct((FFN_HIDDEN, d), BF16)),
        grid=(t // tm, per),
        in_specs=[pl.BlockSpec((tm, d), lambda i, j: (i, 0)),
                  pl.BlockSpec((None, d, tn), lambda i, j: (layer, 0, j)),
                  pl.BlockSpec((None, d, tn), lambda i, j: (layer, 0, per + j)),
                  pl.BlockSpec((None, cast_rows, d), lambda i, j: (layer,) + cast_block(i, j))],
        out_specs=(pl.BlockSpec((tm, tn), lambda i, j: (i, j)),
                   pl.BlockSpec((cast_rows, d), cast_block)),
        compiler_params=_params(("arbitrary", "arbitrary"), blocks, 4 * _nbytes((tm, tn), F32)),
        name="ffn_in",
    )(h, w_ffn_in, w_ffn_in, w_ffn_out)


def _rope_tables(n_lat_rows, n_ctx_rows):
    half = A_HEAD_DIM // 4
    t = jnp.arange(n_lat_rows)
    freqs = ROPE_THETA ** (-jnp.arange(half, dtype=F32) / half)

    def cos_sin(pos):
        ang = pos.astype(F32)[:, None] * freqs[None, :]
        return jnp.cos(ang), jnp.sin(ang)

    cr, sr = cos_sin(t // GRID_W)
    cc, sc = cos_sin(t % GRID_W)
    cos = jnp.concatenate([cr, cr, cc, cc], axis=-1)
    sin = jnp.concatenate([-sr, sr, -sc, sc], axis=-1)
    cos = jnp.concatenate([cos, jnp.ones((n_ctx_rows, A_HEAD_DIM), F32)], axis=0)
    sin = jnp.concatenate([sin, jnp.zeros((n_ctx_rows, A_HEAD_DIM), F32)], axis=0)
    reps = LANES // A_HEAD_DIM
    return jnp.tile(cos, (1, reps)), jnp.tile(sin, (1, reps))


def kernel(x, c, ctx, c_ctx, w_mod, b_mod, norm1, norm2, w_in, b_gate, a_sink, nb_bias, c_conv_w, c_conv_b,
           c_w_a, c_b_a, c_w_x, c_b_x, c_lam, d_conv_w, w_branch, w_out, w_ffn_in, w_ffn_out, final_norm):
    bsz, s, d = x.shape
    l_ctx = ctx.shape[1]
    assert bsz == 1 and d == D_MODEL and c.shape[0] == 1
    assert s % ROW_CHUNK == 0 and l_ctx % ROW_CHUNK == 0 and s // GRID_W >= NB_ROWS and s >= 3 * A_BLOCK
    t = s + l_ctx
    depth = w_mod.shape[0]

    wg_scan = jnp.concatenate([c_w_a, c_w_x], axis=-1).astype(BF16)

    cos_t, sin_t = _rope_tables(s, l_ctx)
    bias_rows = _nb_bias_rows(nb_bias)

    cs = jnp.concatenate([c, c_ctx[None, :], jnp.zeros((SUBLANES - 2, d), F32)], axis=0)
    mod_all = _modulation(cs, w_mod, b_mod)
    mod_all = mod_all[:, :2].reshape(depth, 2, N_MOD, d).transpose(0, 2, 1, 3)


    tm_big = _pick_tile(t, 1408)
    tm_huge = _pick_tile(t, 2816)
    tm_small = _pick_tile(t, 704)
    xt, h = _norm_mod(x[0], ctx[0], norm1[0], mod_all[0], 0)
    for l in range(depth):
        mod = mod_all[l]
        za = _matmul(h, w_in, l, col0=0, n=A_COLS, tm=tm_big, tn=A_COLS, out_dtype=BF16, name="proj_a",
                     rope=(cos_t, sin_t))
        zb = _matmul(h, w_in, l, col0=A_COLS, n=B_COLS, tm=tm_big, tn=A_COLS, out_dtype=BF16, name="proj_b")
        zcd = _matmul(h, w_in, l, col0=A_COLS + B_COLS, n=CD_COLS, tm=tm_huge, tn=256, out_dtype=F32,
                      name="proj_cd")
        ya = _attn_a(za, a_sink[l], s)
        yb = _attn_b(zb, bias_rows, l, s)
        u, yd = _cd_prep(zcd, c_conv_w[l], c_conv_b[l], d_conv_w[l], s)
        hf, hb = _rglru_scan(u, wg_scan[l], c_b_a[l], c_b_x[l], c_lam[l], s)
        yr = _c_final(hf, hb, zcd)
        merged = _merge(h, (ya, yb, yr, yd), w_in, w_branch, b_gate, l, tm=tm_small, tn=256)
        xt, h2 = _resid_norm(merged, w_out, l, xt, mod, 2, norm2[l], mod, 3, s, name="proj_out")
        act, w_ffn_out_b = _ffn_in(h2, w_ffn_in, w_ffn_out, l, tm=tm_big, tn=512)
        if l + 1 < depth:
            xt, h = _resid_norm(act, w_ffn_out_b, None, xt, mod, 5, norm1[l + 1], mod_all[l + 1], 0, s,
                                name="ffn_out")
    out = _resid_final(act, w_ffn_out_b, None, xt, mod_all[depth - 1], 5, final_norm, s, name="ffn_out_final")
    return out[None]
```
